```python
import math
import jax, jax.numpy as jnp
from jax import lax
import numpy as np

D_MODEL = 1024
BATCH = 32
SEQ = 2048
DEPTH = 2
DEC_BATCH = 16
DEC_SEQ = 32
PAST_LEN = 2048

CHUNK = 64
N_MIXERS = 2
N_GLA = (DEPTH + 1) // 2
N_DIFF = DEPTH // 2
GLA_HEADS = 4
GLA_DK = D_MODEL // 2 // GLA_HEADS
GLA_DV = D_MODEL // GLA_HEADS
GLA_RANK = 16
GLA_TAU = 16.0
GLA_IN = 2 * GLA_HEADS * GLA_DK + 2 * GLA_HEADS * GLA_DV + GLA_RANK
DIFF_HEADS = D_MODEL // 128
DIFF_HD = 64
DIFF_DV = 2 * DIFF_HD
DIFF_IN = DIFF_HEADS * (4 * DIFF_HD + DIFF_DV)
Q_BLOCK = 128
D_FF = 2816
CONV_W = 3
EPS = 1e-6

kernel_name = 'hybrid_gla_diffattn_convffn_stream_step'


def rms_norm(x, gain):
    xf = x.astype(jnp.float32)
    y = xf * lax.rsqrt(jnp.mean(xf * xf, axis=-1, keepdims=True) + EPS)
    return (y * gain.astype(jnp.float32)).astype(x.dtype)


def adaln(c, w_ada, b_ada):
    m = jnp.einsum('bd,de->be', jax.nn.silu(c), w_ada) + b_ada
    return jnp.split(m, 6, axis=-1)


def modulate(h, shift, scale):
    return h * (1.0 + scale[:, None, :]) + shift[:, None, :]


def gla_chunk_step(S, inp):
    q, k, v, g = inp
    C = q.shape[1]
    b = jnp.cumsum(g, axis=1)
    qe = q * jnp.exp(b)
    ke = k * jnp.exp(-b)
    causal = jnp.tril(jnp.ones((C, C), dtype=bool))
    att = jnp.where(causal, jnp.einsum('bthd,bshd->bhts', qe, ke), 0.0)
    o = jnp.einsum('bthd,bhde->bthe', qe, S) + jnp.einsum('bhts,bshe->bthe', att, v)
    b_last = b[:, -1]
    k_dec = k * jnp.exp(b_last[:, None] - b)
    S_new = jnp.exp(b_last)[..., None] * S + jnp.einsum('bshd,bshe->bhde', k_dec, v)
    return S_new, o


def gla_mixer(h, S0, w_in, w_alpha, b_alpha, norm_g, w_out):
    B, T, _ = h.shape
    nq = GLA_HEADS * GLA_DK
    nv = GLA_HEADS * GLA_DV
    proj = jnp.einsum('btd,de->bte', h, w_in)
    q, k, v, r, a_low = jnp.split(proj, [nq, 2 * nq, 2 * nq + nv, 2 * nq + 2 * nv], axis=-1)
    logit = jnp.einsum('btr,re->bte', a_low, w_alpha) + b_alpha
    g = jax.nn.log_sigmoid(logit.astype(jnp.float32)) / GLA_TAU
    C = min(CHUNK, T)
    n = T // C

    def chunks(a, dh):
        return a.astype(jnp.float32).reshape(B, n, C, GLA_HEADS, dh).swapaxes(0, 1)

    S_fin, o = lax.scan(gla_chunk_step, S0.astype(jnp.float32),
                        (chunks(q * GLA_DK ** -0.5, GLA_DK), chunks(k, GLA_DK),
                         chunks(v, GLA_DV), chunks(g, GLA_DK)))
    o = o.swapaxes(0, 1).reshape(B, T, GLA_HEADS, GLA_DV)
    o = rms_norm(o, norm_g).astype(h.dtype) * jax.nn.silu(r).reshape(B, T, GLA_HEADS, GLA_DV)
    out = jnp.einsum('bthe,hed->btd', o, w_out.reshape(GLA_HEADS, GLA_DV, D_MODEL))
    return out, S_fin.astype(S0.dtype)


def diff_project(h, w_in, q_gain, k_gain):
    B, T, _ = h.shape
    nqk = DIFF_HEADS * 2 * DIFF_HD
    proj = jnp.einsum('btd,de->bte', h, w_in)
    q, k, v = jnp.split(proj, [nqk, 2 * nqk], axis=-1)
    q = rms_norm(q.reshape(B, T, DIFF_HEADS, 2, DIFF_HD), q_gain)
    k = rms_norm(k.reshape(B, T, DIFF_HEADS, 2, DIFF_HD), k_gain)
    v = v.reshape(B, T, DIFF_HEADS, DIFF_DV)
    return q, k, v


def diff_attend(q, k, v, q_offset, lam):
    B, T = q.shape[:2]
    Tk = k.shape[1]
    qb = min(Q_BLOCK, T)
    nb = T // qb
    q_blocks = q.reshape(B, nb, qb, DIFF_HEADS, 2, DIFF_HD).swapaxes(0, 1)
    k_chunk = jnp.arange(Tk) // CHUNK
    scale = DIFF_HD ** -0.5

    def one_block(args):
        q_blk, j = args
        q_pos = q_offset + j * qb + jnp.arange(qb)
        mask = k_chunk[None, :] <= (q_pos // CHUNK)[:, None]
        s = jnp.einsum('bqhcd,bkhcd->bhcqk', q_blk, k).astype(jnp.float32) * scale
        p = jax.nn.softmax(jnp.where(mask, s, -jnp.inf), axis=-1)
        a = p[:, :, 0] - lam * p[:, :, 1]
        return jnp.einsum('bhqk,bkhe->bqhe', a.astype(v.dtype), v)

    o = lax.map(one_block, (q_blocks, jnp.arange(nb)))
    return o.swapaxes(0, 1).reshape(B, T, DIFF_HEADS, DIFF_DV)


def conv_ffn(h, conv_state, w_in, conv_w, conv_b, w_out):
    T = h.shape[1]
    up = jnp.einsum('btd,de->bte', h, w_in)
    padded = jnp.concatenate([conv_state.astype(up.dtype), up], axis=1)
    conv = conv_b + conv_w[0] * padded[:, 0:T]
    for j in range(1, CONV_W):
        conv = conv + conv_w[j] * padded[:, j:j + T]
    u, gate = jnp.split(conv, 2, axis=-1)
    y = jnp.einsum('btf,fd->btd', jax.nn.silu(gate) * u, w_out)
    return y, padded[:, T:]


def setup_inputs(seed: int = 0) -> dict:
    key = jax.random.key(seed)
    ks = jax.random.split(key, 32)
    f32 = jnp.float32

    def nrm(k, shape, scale):
        return jax.random.normal(k, shape, f32) * scale

    nq = GLA_HEADS * GLA_DK
    nv = GLA_HEADS * GLA_DV
    return {
        'x_prompt': nrm(ks[0], (BATCH, SEQ, D_MODEL), 1.0),
        'x_sample': nrm(ks[1], (DEC_BATCH, DEC_SEQ, D_MODEL), 1.0),
        'state_gla': nrm(ks[2], (N_GLA, DEC_BATCH, GLA_HEADS, GLA_DK, GLA_DV), 0.5),
        'cache_diff_k': nrm(ks[3], (N_DIFF, DEC_BATCH, PAST_LEN, DIFF_HEADS, 2, DIFF_HD), 1.0),
        'cache_diff_v': nrm(ks[4], (N_DIFF, DEC_BATCH, PAST_LEN, DIFF_HEADS, DIFF_DV), 1.0),
        'state_ffn_conv': nrm(ks[5], (DEPTH, DEC_BATCH, CONV_W - 1, 2 * D_FF), 0.5),
        'c_prompt': nrm(ks[6], (BATCH, D_MODEL), 1.0),
        'c_sample': nrm(ks[7], (DEC_BATCH, D_MODEL), 1.0),
        'norm_mix': 1.0 + nrm(ks[8], (DEPTH, D_MODEL), 0.01),
        'norm_ffn': 1.0 + nrm(ks[9], (DEPTH, D_MODEL), 0.01),
        'w_ada': nrm(ks[10], (DEPTH, D_MODEL, 6 * D_MODEL), 0.5 * D_MODEL ** -0.5),
        'b_ada': nrm(ks[11], (DEPTH, 6 * D_MODEL), 0.01),
        'gla_w_in': nrm(ks[12], (N_GLA, D_MODEL, GLA_IN), D_MODEL ** -0.5),
        'gla_w_alpha': nrm(ks[13], (N_GLA, GLA_RANK, nq), GLA_RANK ** -0.5),
        'gla_b_alpha': nrm(ks[14], (N_GLA, nq), 0.1),
        'gla_norm': 1.0 + nrm(ks[15], (N_GLA, GLA_DV), 0.01),
        'gla_w_out': nrm(ks[16], (N_GLA, nv, D_MODEL), nv ** -0.5),
        'diff_w_in': nrm(ks[17], (N_DIFF, D_MODEL, DIFF_IN), D_MODEL ** -0.5),
        'diff_q_norm': 1.0 + nrm(ks[18], (N_DIFF, DIFF_HD), 0.01),
        'diff_k_norm': 1.0 + nrm(ks[19], (N_DIFF, DIFF_HD), 0.01),
        'diff_lambda': nrm(ks[20], (N_DIFF, 4, DIFF_HD), 0.1),
        'diff_norm': 1.0 + nrm(ks[21], (N_DIFF, DIFF_DV), 0.01),
        'diff_w_out': nrm(ks[22], (N_DIFF, DIFF_HEADS * DIFF_DV, D_MODEL), (DIFF_HEADS * DIFF_DV) ** -0.5),
        'ffn_w_in': nrm(ks[23], (DEPTH, D_MODEL, 2 * D_FF), D_MODEL ** -0.5),
        'ffn_conv_w': nrm(ks[24], (DEPTH, CONV_W, 2 * D_FF), CONV_W ** -0.5),
        'ffn_conv_b': nrm(ks[25], (DEPTH, 2 * D_FF), 0.01),
        'ffn_w_out': nrm(ks[26], (DEPTH, D_FF, D_MODEL), D_FF ** -0.5),
    }


def reference(x_prompt, x_sample, state_gla, cache_diff_k, cache_diff_v, state_ffn_conv,
              c_prompt, c_sample, norm_mix, norm_ffn, w_ada, b_ada,
              gla_w_in, gla_w_alpha, gla_b_alpha, gla_norm, gla_w_out,
              diff_w_in, diff_q_norm, diff_k_norm, diff_lambda, diff_norm, diff_w_out,
              ffn_w_in, ffn_conv_w, ffn_conv_b, ffn_w_out):
    xs = [x_prompt, x_sample]
    cs = [c_prompt, c_sample]
    new_gla = [[], []]
    new_k = [[], []]
    new_v = [[], []]
    new_conv = [[], []]
    for i in range(DEPTH):
        j = i // N_MIXERS
        lam_init = 0.8 - 0.6 * math.exp(-0.3 * i)
        for grp in range(2):
            x = xs[grp]
            B = x.shape[0]
            sh1, sc1, gt1, sh2, sc2, gt2 = adaln(cs[grp], w_ada[i], b_ada[i])
            h = modulate(rms_norm(x, norm_mix[i]), sh1, sc1)
            if i % N_MIXERS == 0:
                S0 = jnp.zeros((B, GLA_HEADS, GLA_DK, GLA_DV), x.dtype) if grp == 0 else state_gla[j]
                mix, S_fin = gla_mixer(h, S0, gla_w_in[j], gla_w_alpha[j], gla_b_alpha[j],
                                       gla_norm[j], gla_w_out[j])
                new_gla[grp].append(S_fin)
            else:
                q, k, v = diff_project(h, diff_w_in[j], diff_q_norm[j], diff_k_norm[j])
                lq1, lk1, lq2, lk2 = diff_lambda[j].astype(jnp.float32)
                lam = jnp.exp(jnp.sum(lq1 * lk1)) - jnp.exp(jnp.sum(lq2 * lk2)) + lam_init
                if grp == 0:
                    o = diff_attend(q, k, v, 0, lam)
                else:
                    k_all = jnp.concatenate([cache_diff_k[j].astype(k.dtype), k], axis=1)
                    v_all = jnp.concatenate([cache_diff_v[j].astype(v.dtype), v], axis=1)
                    o = diff_attend(q, k_all, v_all, PAST_LEN, lam)
                o = rms_norm(o, diff_norm[j]) * (1.0 - lam_init)
                mix = jnp.einsum('bthe,hed->btd', o, diff_w_out[j].reshape(DIFF_HEADS, DIFF_DV, D_MODEL))
                new_k[grp].append(k)
                new_v[grp].append(v)
            x = x + gt1[:, None, :] * mix
            h = modulate(rms_norm(x, norm_ffn[i]), sh2, sc2)
            cst = jnp.zeros((B, CONV_W - 1, 2 * D_FF), x.dtype) if grp == 0 else state_ffn_conv[i]
            f, cst_new = conv_ffn(h, cst, ffn_w_in[i], ffn_conv_w[i], ffn_conv_b[i], ffn_w_out[i])
            new_conv[grp].append(cst_new)
            xs[grp] = x + gt2[:, None, :] * f
    y_prompt = xs[0]
    y_sample = xs[1]
    new_state_gla_prompt = jnp.stack(new_gla[0])
    new_state_gla_sample = jnp.stack(new_gla[1])
    new_k_prompt = jnp.stack(new_k[0])
    new_k_sample = jnp.stack(new_k[1])
    new_v_prompt = jnp.stack(new_v[0])
    new_v_sample = jnp.stack(new_v[1])
    new_conv_prompt = jnp.stack(new_conv[0])
    new_conv_sample = jnp.stack(new_conv[1])
    return (y_prompt, y_sample, new_state_gla_prompt, new_state_gla_sample,
            new_k_prompt, new_k_sample, new_v_prompt, new_v_sample,
            new_conv_prompt, new_conv_sample)
```

```python
import functools
import math

import jax
import jax.numpy as jnp
from jax import lax
from jax.experimental import pallas as pl
from jax.experimental.pallas import tpu as pltpu

F32 = jnp.float32
BF16 = jnp.bfloat16

EPS = 1e-6
CHUNK = 64
GLA_TAU = 16.0
LANES = 128
MXU_DIM = 256
NEG = -1e30
VMEM_LIMIT = 56 * 1024 * 1024

NT = (((1,), (1,)), ((), ()))
TN = (((0,), (0,)), ((), ()))


def _dot(a, b):
    return jnp.dot(a, b, preferred_element_type=F32)


def _silu(x):
    return x * (1.0 / (1.0 + jnp.exp(-x)))


def _norm_mod(x, gain, shift, scale):
    ms = jnp.mean(x * x, axis=-1, keepdims=True)
    y = x * lax.rsqrt(ms + EPS) * gain
    return y * (1.0 + scale) + shift


def _resident(shape):
    zeros = (0,) * len(shape)
    return pl.BlockSpec(shape, lambda *_: zeros, pipeline_mode=pl.Buffered(1))


def _params(*sem):
    return pltpu.CompilerParams(dimension_semantics=sem, vmem_limit_bytes=VMEM_LIMIT)


def _adaln_kernel(c_ref, w_ref, b_ref, o_ref):
    s = _silu(c_ref[...]).astype(BF16)
    o_ref[0] = _dot(s, w_ref[0].astype(BF16)) + b_ref[0]


def _adaln(c, w, b):
    depth, d, n = w.shape
    rows = c.shape[0]
    tn = n // 4
    return pl.pallas_call(
        _adaln_kernel,
        out_shape=jax.ShapeDtypeStruct((depth, rows, n), F32),
        grid=(depth, n // tn),
        in_specs=[
            pl.BlockSpec((rows, d), lambda i, j: (0, 0)),
            pl.BlockSpec((1, d, tn), lambda i, j: (i, 0, j)),
            pl.BlockSpec((1, 1, tn), lambda i, j: (i, 0, j)),
        ],
        out_specs=pl.BlockSpec((1, rows, tn), lambda i, j: (i, 0, j)),
        compiler_params=_params("arbitrary", "arbitrary"),
        name="adaln",
    )(c, w, b.reshape(depth, 1, n))


def _gla_kernel(x_ref, mod_ref, ng_ref, wm_ref, wa_ref, wal_ref, bal_ref, gn_ref, wo_ref, s0_ref,
                y_ref, sout_ref, proj_ref, g_ref, og_ref, st_ref, *, chunk, heads, dk, dv):
    t = pl.program_id(1)
    tb = x_ref.shape[1]
    nq, nv = heads * dk, heads * dv

    @pl.when(t == 0)
    def _():
        for hh in range(heads):
            st_ref[hh] = s0_ref[0, hh].T

    x = x_ref[0]
    mod = mod_ref[0]
    hb = _norm_mod(x, ng_ref[...], mod[0:1], mod[1:2]).astype(BF16)
    proj_ref[...] = _dot(hb, wm_ref[...])
    alow = _dot(hb, wa_ref[...])
    logit = _dot(alow.astype(BF16), wal_ref[...]) + bal_ref[...]
    g_ref[...] = (jnp.minimum(logit, 0.0) - jnp.log1p(jnp.exp(-jnp.abs(logit)))) / GLA_TAU

    row = lax.broadcasted_iota(jnp.int32, (chunk, chunk), 0)
    col = lax.broadcasted_iota(jnp.int32, (chunk, chunk), 1)
    causal = row >= col
    tril = jnp.where(causal, 1.0, 0.0).astype(BF16)
    q_scale = dk ** -0.5

    for c in range(tb // chunk):
        rows = pl.ds(c * chunk, chunk)
        g = g_ref[rows, :]
        g1 = g.astype(BF16)
        r1 = g - g1.astype(F32)
        g2 = r1.astype(BF16)
        g3 = (r1 - g2.astype(F32)).astype(BF16)
        b = _dot(tril, g1) + _dot(tril, g2) + _dot(tril, g3)
        for hh in range(heads):
            bh = b[:, hh * dk:(hh + 1) * dk]
            q = proj_ref[rows, hh * dk:(hh + 1) * dk] * q_scale
            k = proj_ref[rows, nq + hh * dk:nq + (hh + 1) * dk]
            v = proj_ref[rows, 2 * nq + hh * dv:2 * nq + (hh + 1) * dv].astype(BF16)
            r = proj_ref[rows, 2 * nq + nv + hh * dv:2 * nq + nv + (hh + 1) * dv]
            qe = (q * jnp.exp(bh)).astype(BF16)
            ke = (k * jnp.exp(-bh)).astype(BF16)
            att = jnp.where(causal, lax.dot_general(qe, ke, NT, preferred_element_type=F32), 0.0)
            s_t = st_ref[hh]
            o = lax.dot_general(qe, s_t.astype(BF16), NT, preferred_element_type=F32) + _dot(att.astype(BF16), v)
            b_last = bh[chunk - 1:chunk, :]
            k_dec = (k * jnp.exp(b_last - bh)).astype(BF16)
            st_ref[hh] = s_t * jnp.exp(b_last) + lax.dot_general(v, k_dec, TN, preferred_element_type=F32)
            ms = jnp.mean(o * o, axis=-1, keepdims=True)
            on = o * lax.rsqrt(ms + EPS) * gn_ref[...]
            og_ref[rows, hh * dv:(hh + 1) * dv] = (on * _silu(r)).astype(BF16)

    y_ref[0] = x + mod[2:3] * _dot(og_ref[...], wo_ref[...])

    @pl.when(t == pl.num_programs(1) - 1)
    def _():
        for hh in range(heads):
            sout_ref[0, hh] = st_ref[hh].T


def _gla_layer(x, mod, norm_g, w_main, w_a, w_alpha, b_alpha, gla_norm, w_out, s0, *, heads, dk, dv):
    bsz, t, d = x.shape
    tb = min(t, 256)
    chunk = min(CHUNK, t)
    nq, nv = heads * dk, heads * dv
    kern = functools.partial(_gla_kernel, chunk=chunk, heads=heads, dk=dk, dv=dv)
    return pl.pallas_call(
        kern,
        out_shape=(jax.ShapeDtypeStruct((bsz, t, d), F32),
                   jax.ShapeDtypeStruct((bsz, heads, dk, dv), F32)),
        grid=(bsz, t // tb),
        in_specs=[
            pl.BlockSpec((1, tb, d), lambda b, i: (b, i, 0)),
            pl.BlockSpec((1, 6, d), lambda b, i: (b, 0, 0)),
            _resident((1, d)),
            _resident(w_main.shape),
            _resident(w_a.shape),
            _resident(w_alpha.shape),
            _resident((1, nq)),
            _resident((1, dv)),
            _resident(w_out.shape),
            pl.BlockSpec((1, heads, dk, dv), lambda b, i: (b, 0, 0, 0)),
        ],
        out_specs=(pl.BlockSpec((1, tb, d), lambda b, i: (b, i, 0)),
                   pl.BlockSpec((1, heads, dk, dv), lambda b, i: (b, 0, 0, 0))),
        scratch_shapes=[
            pltpu.VMEM((tb, 2 * nq + 2 * nv), F32),
            pltpu.VMEM((tb, nq), F32),
            pltpu.VMEM((tb, nv), BF16),
            pltpu.VMEM((heads, dv, dk), F32),
        ],
        compiler_params=_params("arbitrary", "arbitrary"),
        name="gla_mixer",
    )(x, mod, norm_g.reshape(1, d), w_main, w_a, w_alpha, b_alpha.reshape(1, nq),
      gla_norm.reshape(1, dv), w_out, s0)


def _ffn_kernel(x_ref, mod_ref, ng_ref, wi_ref, cw_ref, cb_ref, wo_ref, cs_ref,
                y_ref, cso_ref, act_ref, *, ff, fc):
    t = pl.program_id(1)
    tb = x_ref.shape[1]

    @pl.when(t == 0)
    def _():
        cso_ref[...] = cs_ref[...]

    x = x_ref[0]
    mod = mod_ref[0]
    hb = _norm_mod(x, ng_ref[...], mod[3:4], mod[4:5]).astype(BF16)
    row = lax.broadcasted_iota(jnp.int32, (tb, fc), 0)
    is0 = row == 0
    is1 = row == 1
    for f0 in range(0, ff, fc):
        conv = []
        for off in (f0, ff + f0):
            cols = slice(off, off + fc)
            up = _dot(hb, wi_ref[:, cols])
            prev = cso_ref[0, :, cols]
            p1 = jnp.where(is0, prev[1:2], pltpu.roll(up, 1, 0))
            p2 = jnp.where(is0, prev[0:1], jnp.where(is1, prev[1:2], pltpu.roll(up, 2, 0)))
            cw = cw_ref[:, cols]
            cv = cb_ref[:, cols] + cw[0:1] * p2
            cv = cv + cw[1:2] * p1
            cv = cv + cw[2:3] * up
            cso_ref[0, :, cols] = up[tb - 2:tb, :]
            conv.append(cv)
        u, gate = conv
        act_ref[:, f0:f0 + fc] = (_silu(gate) * u).astype(BF16)
    y_ref[0] = x + mod[5:6] * _dot(act_ref[...], wo_ref[...])


def _ffn_layer(x, mod, norm_g, w_in, conv_w, conv_b, w_out, conv_state):
    bsz, t, d = x.shape
    ff = w_out.shape[0]
    tb = min(t, 512)
    fc = MXU_DIM
    kern = functools.partial(_ffn_kernel, ff=ff, fc=fc)
    return pl.pallas_call(
        kern,
        out_shape=(jax.ShapeDtypeStruct((bsz, t, d), F32),
                   jax.ShapeDtypeStruct(conv_state.shape, F32)),
        grid=(bsz, t // tb),
        in_specs=[
            pl.BlockSpec((1, tb, d), lambda b, i: (b, i, 0)),
            pl.BlockSpec((1, 6, d), lambda b, i: (b, 0, 0)),
            _resident((1, d)),
            _resident(w_in.shape),
            _resident(conv_w.shape),
            _resident((1, 2 * ff)),
            _resident(w_out.shape),
            pl.BlockSpec((1, 2, 2 * ff), lambda b, i: (b, 0, 0)),
        ],
        out_specs=(pl.BlockSpec((1, tb, d), lambda b, i: (b, i, 0)),
                   pl.BlockSpec((1, 2, 2 * ff), lambda b, i: (b, 0, 0))),
        scratch_shapes=[pltpu.VMEM((tb, ff), BF16)],
        compiler_params=_params("arbitrary", "arbitrary"),
        name="conv_ffn",
    )(x, mod, norm_g.reshape(1, d), w_in, conv_w, conv_b.reshape(1, 2 * ff), w_out, conv_state)


def _dproj_kernel(x_ref, mod_ref, ng_ref, w_ref, qg_ref, kg_ref, gm_ref,
                  qb_ref, kb_ref, vb_ref, k_ref, v_ref, *, hd):
    x = x_ref[0]
    mod = mod_ref[0]
    hb = _norm_mod(x, ng_ref[...], mod[0:1], mod[1:2]).astype(BF16)
    nqk = qg_ref.shape[1]

    def group_norm(a, gain):
        sq = (a * a).astype(BF16)
        outs = []
        for g0 in range(0, nqk, MXU_DIM):
            ms = _dot(sq[:, g0:g0 + MXU_DIM], gm_ref[...])
            outs.append(a[:, g0:g0 + MXU_DIM] * lax.rsqrt(ms + EPS) * gain[:, g0:g0 + MXU_DIM])
        return jnp.concatenate(outs, axis=-1)

    qn = group_norm(_dot(hb, w_ref[:, 0:nqk]), qg_ref[...])
    qb_ref[0] = (qn * (hd ** -0.5)).astype(BF16)
    kn = group_norm(_dot(hb, w_ref[:, nqk:2 * nqk]), kg_ref[...])
    k_ref[0] = kn
    kb_ref[0] = kn.astype(BF16)
    v = _dot(hb, w_ref[:, 2 * nqk:])
    v_ref[0] = v
    vb_ref[0] = v.astype(BF16)


def _dproj_layer(x, mod, norm_g, w_in, q_gain, k_gain, gmat, *, hd):
    bsz, t, d = x.shape
    nqk = q_gain.shape[1]
    nv = w_in.shape[1] - 2 * nqk
    tb = min(t, 512)
    blk = lambda n: pl.BlockSpec((1, tb, n), lambda b, i: (b, i, 0))
    return pl.pallas_call(
        functools.partial(_dproj_kernel, hd=hd),
        out_shape=(jax.ShapeDtypeStruct((bsz, t, nqk), BF16),
                   jax.ShapeDtypeStruct((bsz, t, nqk), BF16),
                   jax.ShapeDtypeStruct((bsz, t, nv), BF16),
                   jax.ShapeDtypeStruct((bsz, t, nqk), F32),
                   jax.ShapeDtypeStruct((bsz, t, nv), F32)),
        grid=(bsz, t // tb),
        in_specs=[
            blk(d),
            pl.BlockSpec((1, 6, d), lambda b, i: (b, 0, 0)),
            _resident((1, d)),
            _resident(w_in.shape),
            _resident((1, nqk)),
            _resident((1, nqk)),
            _resident(gmat.shape),
        ],
        out_specs=(blk(nqk), blk(nqk), blk(nv), blk(nqk), blk(nv)),
        compiler_params=_params("arbitrary", "arbitrary"),
        name="diff_proj",
    )(x, mod, norm_g.reshape(1, d), w_in, q_gain, k_gain, gmat)


def _attn_init(m_ref, l_ref, acc_ref):
    m_ref[...] = jnp.full(m_ref.shape, NEG, F32)
    l_ref[...] = jnp.zeros(l_ref.shape, F32)
    acc_ref[...] = jnp.zeros(acc_ref.shape, F32)


def _attn_block(q_ref, kv, m_ref, l_ref, acc_ref, *, heads, hd, dv, mask):
    tq = q_ref.shape[1]
    lane = lax.broadcasted_iota(jnp.int32, (tq, 2 * hd), 1)
    for h in range(heads):
        qh = q_ref[0, :, h * 2 * hd:(h + 1) * 2 * hd]
        kh, vh = kv(h)
        for c in range(2):
            sub = (lane < hd) if c == 0 else (lane >= hd)
            qc = jnp.where(sub, qh, jnp.zeros_like(qh))
            s = lax.dot_general(qc, kh, NT, preferred_element_type=F32)
            if mask is not None:
                s = jnp.where(mask, s, NEG)
            i = 2 * h + c
            m_old = m_ref[i]
            m_new = jnp.maximum(m_old, jnp.max(s, axis=-1, keepdims=True))
            alpha = jnp.exp(m_old - m_new)
            p = jnp.exp(s - m_new)
            l_ref[i] = alpha * l_ref[i] + jnp.sum(p, axis=-1, keepdims=True)
            acc_ref[i] = alpha * acc_ref[i] + _dot(p.astype(BF16), vh)
            m_ref[i] = m_new


def _attn_finish(x_ref, mod_ref, lam_ref, dn_ref, wo_ref, y_ref, l_ref, acc_ref, ob_ref,
                 *, heads, dv, lam_init):
    lp = lam_ref[...]
    lam = (jnp.exp(jnp.sum(lp[0:1] * lp[1:2], axis=-1, keepdims=True))
           - jnp.exp(jnp.sum(lp[2:3] * lp[3:4], axis=-1, keepdims=True)) + lam_init)
    for h in range(heads):
        o = acc_ref[2 * h] * (1.0 / l_ref[2 * h]) - lam * (acc_ref[2 * h + 1] * (1.0 / l_ref[2 * h + 1]))
        ms = jnp.mean(o * o, axis=-1, keepdims=True)
        on = o * lax.rsqrt(ms + EPS) * dn_ref[...] * (1.0 - lam_init)
        ob_ref[:, h * dv:(h + 1) * dv] = on.astype(BF16)
    y_ref[0] = x_ref[0] + mod_ref[0][2:3] * _dot(ob_ref[...], wo_ref[...])


def _attn_self_kernel(q_ref, k_ref, v_ref, x_ref, mod_ref, lam_ref, dn_ref, wo_ref,
                      y_ref, m_ref, l_ref, acc_ref, ob_ref, *, heads, hd, dv, lam_init):
    j = pl.program_id(1)
    tq = q_ref.shape[1]
    _attn_init(m_ref, l_ref, acc_ref)

    def kv_at(start):
        def kv(h):
            return (k_ref[0, pl.ds(start, tq), h * 2 * hd:(h + 1) * 2 * hd],
                    v_ref[0, pl.ds(start, tq), h * dv:(h + 1) * dv])
        return kv

    def body(kb, carry):
        _attn_block(q_ref, kv_at(pl.multiple_of(kb * tq, tq)), m_ref, l_ref, acc_ref,
                    heads=heads, hd=hd, dv=dv, mask=None)
        return carry

    lax.fori_loop(0, j, body, 0)
    row = lax.broadcasted_iota(jnp.int32, (tq, tq), 0)
    col = lax.broadcasted_iota(jnp.int32, (tq, tq), 1)
    mask = (col // CHUNK) <= (row // CHUNK)
    _attn_block(q_ref, kv_at(pl.multiple_of(j * tq, tq)), m_ref, l_ref, acc_ref,
                heads=heads, hd=hd, dv=dv, mask=mask)
    _attn_finish(x_ref, mod_ref, lam_ref, dn_ref, wo_ref, y_ref, l_ref, acc_ref, ob_ref,
                 heads=heads, dv=dv, lam_init=lam_init)


def _attn_cached_kernel(q_ref, kc_ref, vc_ref, kn_ref, vn_ref, x_ref, mod_ref, lam_ref, dn_ref, wo_ref,
                        y_ref, m_ref, l_ref, acc_ref, ob_ref, *, heads, hd, dv, lam_init):
    kb = pl.program_id(1)

    @pl.when(kb == 0)
    def _():
        _attn_init(m_ref, l_ref, acc_ref)

    def kv_cache(h):
        return (kc_ref[0, :, h * 2 * hd:(h + 1) * 2 * hd].astype(BF16),
                vc_ref[0, :, h * dv:(h + 1) * dv].astype(BF16))

    _attn_block(q_ref, kv_cache, m_ref, l_ref, acc_ref, heads=heads, hd=hd, dv=dv, mask=None)

    @pl.when(kb == pl.num_programs(1) - 1)
    def _():
        def kv_new(h):
            return (kn_ref[0, :, h * 2 * hd:(h + 1) * 2 * hd], vn_ref[0, :, h * dv:(h + 1) * dv])

        _attn_block(q_ref, kv_new, m_ref, l_ref, acc_ref, heads=heads, hd=hd, dv=dv, mask=None)
        _attn_finish(x_ref, mod_ref, lam_ref, dn_ref, wo_ref, y_ref, l_ref, acc_ref, ob_ref,
                     heads=heads, dv=dv, lam_init=lam_init)


def _attn_scratch(tq, heads, dv):
    return [pltpu.VMEM((2 * heads, tq, 1), F32), pltpu.VMEM((2 * heads, tq, 1), F32),
            pltpu.VMEM((2 * heads, tq, dv), F32), pltpu.VMEM((tq, heads * dv), BF16)]


def _attn_self_layer(qb, kb, vb, x, mod, lam_p, dnorm, w_out, *, heads, hd, dv, lam_init):
    bsz, t, d = x.shape
    tq = min(t, 256)
    assert t % tq == 0 and tq % CHUNK == 0
    kern = functools.partial(_attn_self_kernel, heads=heads, hd=hd, dv=dv, lam_init=lam_init)
    whole = lambda n: pl.BlockSpec((1, t, n), lambda b, i: (b, 0, 0))
    return pl.pallas_call(
        kern,
        out_shape=jax.ShapeDtypeStruct((bsz, t, d), F32),
        grid=(bsz, t // tq),
        in_specs=[
            pl.BlockSpec((1, tq, qb.shape[2]), lambda b, i: (b, i, 0)),
            whole(kb.shape[2]),
            whole(vb.shape[2]),
            pl.BlockSpec((1, tq, d), lambda b, i: (b, i, 0)),
            pl.BlockSpec((1, 6, d), lambda b, i: (b, 0, 0)),
            _resident(lam_p.shape),
            _resident((1, dv)),
            _resident(w_out.shape),
        ],
        out_specs=pl.BlockSpec((1, tq, d), lambda b, i: (b, i, 0)),
        scratch_shapes=_attn_scratch(tq, heads, dv),
        compiler_params=_params("arbitrary", "arbitrary"),
        name="diff_attn_self",
    )(qb, kb, vb, x, mod, lam_p, dnorm.reshape(1, dv), w_out)


def _attn_cached_layer(qb, k_cache, v_cache, kb, vb, x, mod, lam_p, dnorm, w_out, *, heads, hd, dv, lam_init):
    bsz, t, d = x.shape
    past = k_cache.shape[1]
    tk = min(past, 256)
    assert past % tk == 0 and past % CHUNK == 0 and t <= CHUNK
    kern = functools.partial(_attn_cached_kernel, heads=heads, hd=hd, dv=dv, lam_init=lam_init)
    per_seq = lambda n: pl.BlockSpec((1, t, n), lambda b, i: (b, 0, 0))
    cache = lambda n: pl.BlockSpec((1, tk, n), lambda b, i: (b, i, 0))
    return pl.pallas_call(
        kern,
        out_shape=jax.ShapeDtypeStruct((bsz, t, d), F32),
        grid=(bsz, past // tk),
        in_specs=[
            per_seq(qb.shape[2]),
            cache(k_cache.shape[2]),
            cache(v_cache.shape[2]),
            per_seq(kb.shape[2]),
            per_seq(vb.shape[2]),
            per_seq(d),
            pl.BlockSpec((1, 6, d), lambda b, i: (b, 0, 0)),
            _resident(lam_p.shape),
            _resident((1, dv)),
            _resident(w_out.shape),
        ],
        out_specs=per_seq(d),
        scratch_shapes=_attn_scratch(t, heads, dv),
        compiler_params=_params("arbitrary", "arbitrary"),
        name="diff_attn_cached",
    )(qb, k_cache, v_cache, kb, vb, x, mod, lam_p, dnorm.reshape(1, dv), w_out)


def kernel(x_prompt, x_sample, state_gla, cache_diff_k, cache_diff_v, state_ffn_conv, c_prompt, c_sample,
           norm_mix, norm_ffn, w_ada, b_ada, gla_w_in, gla_w_alpha, gla_b_alpha, gla_norm, gla_w_out,
           diff_w_in, diff_q_norm, diff_k_norm, diff_lambda, diff_norm, diff_w_out,
           ffn_w_in, ffn_conv_w, ffn_conv_b, ffn_w_out):
    depth, d = norm_mix.shape
    bp, bs = x_prompt.shape[0], x_sample.shape[0]
    ff = ffn_w_out.shape[1]

    rank, nq = gla_w_alpha.shape[1:]
    g_dv = gla_norm.shape[1]
    g_heads = gla_w_out.shape[1] // g_dv
    g_dk = nq // g_heads
    n_main = gla_w_in.shape[2] - rank
    hd = diff_q_norm.shape[1]
    a_dv = diff_norm.shape[1]
    a_heads = diff_w_out.shape[1] // a_dv
    nqk = a_heads * 2 * hd
    assert 2 * hd == LANES and a_dv == LANES and MXU_DIM % hd == 0

    mods = _adaln(jnp.concatenate([c_prompt, c_sample], axis=0), w_ada, b_ada)
    mods = mods.reshape(depth, bp + bs, 6, d)
    group_rows = (slice(0, bp), slice(bp, bp + bs))

    gi = jnp.arange(MXU_DIM) // hd
    gmat = jnp.where(gi[:, None] == gi[None, :], 1.0 / hd, 0.0).astype(BF16)

    xs = [x_prompt, x_sample]
    new_gla, new_k, new_v, new_conv = [[], []], [[], []], [[], []], [[], []]
    for i in range(depth):
        j = i // 2
        lam_init = 0.8 - 0.6 * math.exp(-0.3 * i)
        f_in = ffn_w_in[i].astype(BF16)
        f_out = ffn_w_out[i].astype(BF16)
        if i % 2 == 0:
            w_main = gla_w_in[j][:, :n_main].astype(BF16)
            w_a = jnp.pad(gla_w_in[j][:, n_main:], ((0, 0), (0, LANES - rank))).astype(BF16)
            w_alpha = jnp.pad(gla_w_alpha[j], ((0, LANES - rank), (0, 0))).astype(BF16)
            m_out = gla_w_out[j].astype(BF16)
        else:
            m_in = diff_w_in[j].astype(BF16)
            m_out = diff_w_out[j].astype(BF16)
            q_gain = jnp.tile(diff_q_norm[j], 2 * a_heads).reshape(1, nqk)
            k_gain = jnp.tile(diff_k_norm[j], 2 * a_heads).reshape(1, nqk)
        for grp in range(2):
            x = xs[grp]
            bsz, t, _ = x.shape
            mod = mods[i, group_rows[grp]]
            if i % 2 == 0:
                s0 = jnp.zeros((bsz, g_heads, g_dk, g_dv), F32) if grp == 0 else state_gla[j]
                x, s_fin = _gla_layer(x, mod, norm_mix[i], w_main, w_a, w_alpha, gla_b_alpha[j], gla_norm[j],
                                      m_out, s0, heads=g_heads, dk=g_dk, dv=g_dv)
                new_gla[grp].append(s_fin)
            else:
                qb, kb, vb, k, v = _dproj_layer(x, mod, norm_mix[i], m_in, q_gain, k_gain, gmat, hd=hd)
                common = dict(heads=a_heads, hd=hd, dv=a_dv, lam_init=lam_init)
                if grp == 0:
                    x = _attn_self_layer(qb, kb, vb, x, mod, diff_lambda[j], diff_norm[j], m_out, **common)
                else:
                    past = cache_diff_k.shape[2]
                    x = _attn_cached_layer(qb, cache_diff_k[j].reshape(bsz, past, nqk),
                                           cache_diff_v[j].reshape(bsz, past, a_heads * a_dv),
                                           kb, vb, x, mod, diff_lambda[j], diff_norm[j], m_out, **common)
                new_k[grp].append(k.reshape(bsz, t, a_heads, 2, hd))
                new_v[grp].append(v.reshape(bsz, t, a_heads, a_dv))
            cst = jnp.zeros((bsz, 2, 2 * ff), F32) if grp == 0 else state_ffn_conv[i]
            x, cst_new = _ffn_layer(x, mod, norm_ffn[i], f_in, ffn_conv_w[i], ffn_conv_b[i], f_out, cst)
            new_conv[grp].append(cst_new)
            xs[grp] = x

    return (xs[0], xs[1],
            jnp.stack(new_gla[0]), jnp.stack(new_gla[1]),
            jnp.stack(new_k[0]), jnp.stack(new_k[1]),
            jnp.stack(new_v[0]), jnp.stack(new_v[1]),
            jnp.stack(new_conv[0]), jnp.stack(new_conv[1]))
```

```python
import functools
import math

import jax
import jax.numpy as jnp
from jax import lax
from jax.experimental import pallas as pl
from jax.experimental.pallas import tpu as pltpu

F32 = jnp.float32
BF16 = jnp.bfloat16

EPS = 1e-6
CHUNK = 64
GLA_TAU = 16.0
LANES = 128
MXU_DIM = 256
NEG = -1e30
VMEM_LIMIT = 56 * 1024 * 1024

NT = (((1,), (1,)), ((), ()))
TN = (((0,), (0,)), ((), ()))


def _dot(a, b):
    return jnp.dot(a, b, preferred_element_type=F32)


def _silu(x):
    return x * (1.0 / (1.0 + jnp.exp(-x)))


def _norm_mod(x, gain, shift, scale):
    ms = jnp.mean(x * x, axis=-1, keepdims=True)
    y = x * lax.rsqrt(ms + EPS) * gain
    return y * (1.0 + scale) + shift


def _resident(shape):
    zeros = (0,) * len(shape)
    return pl.BlockSpec(shape, lambda *_: zeros, pipeline_mode=pl.Buffered(1))


def _params(*sem):
    return pltpu.CompilerParams(dimension_semantics=sem, vmem_limit_bytes=VMEM_LIMIT)


def _adaln_kernel(c_ref, w_ref, b_ref, o_ref):
    s = _silu(c_ref[...]).astype(BF16)
    o_ref[0] = _dot(s, w_ref[0].astype(BF16)) + b_ref[0]


def _adaln(c, w, b):
    depth, d, n = w.shape
    rows = c.shape[0]
    tn = n // 4
    return pl.pallas_call(
        _adaln_kernel,
        out_shape=jax.ShapeDtypeStruct((depth, rows, n), F32),
        grid=(depth, n // tn),
        in_specs=[
            pl.BlockSpec((rows, d), lambda i, j: (0, 0)),
            pl.BlockSpec((1, d, tn), lambda i, j: (i, 0, j)),
            pl.BlockSpec((1, 1, tn), lambda i, j: (i, 0, j)),
        ],
        out_specs=pl.BlockSpec((1, rows, tn), lambda i, j: (i, 0, j)),
        compiler_params=_params("arbitrary", "arbitrary"),
        name="adaln",
    )(c, w, b.reshape(depth, 1, n))


def _gla_kernel(x_ref, mod_ref, ng_ref, wm_ref, wa_ref, wal_ref, bal_ref, gn_ref, wo_ref, s0_ref,
                y_ref, sout_ref, proj_ref, g_ref, og_ref, st_ref, *, chunk, heads, dk, dv):
    t = pl.program_id(1)
    tb = x_ref.shape[1]
    nq, nv = heads * dk, heads * dv

    @pl.when(t == 0)
    def _():
        for hh in range(heads):
            st_ref[hh] = s0_ref[0, hh].T

    x = x_ref[0]
    mod = mod_ref[0]
    hb = _norm_mod(x, ng_ref[...], mod[0:1], mod[1:2]).astype(BF16)
    proj_ref[...] = _dot(hb, wm_ref[...])
    alow = _dot(hb, wa_ref[...])
    logit = _dot(alow.astype(BF16), wal_ref[...]) + bal_ref[...]
    g_ref[...] = (jnp.minimum(logit, 0.0) - jnp.log1p(jnp.exp(-jnp.abs(logit)))) / GLA_TAU

    row = lax.broadcasted_iota(jnp.int32, (chunk, chunk), 0)
    col = lax.broadcasted_iota(jnp.int32, (chunk, chunk), 1)
    causal = row >= col
    tril = jnp.where(causal, 1.0, 0.0).astype(BF16)
    q_scale = dk ** -0.5

    for c in range(tb // chunk):
        rows = pl.ds(c * chunk, chunk)
        g = g_ref[rows, :]
        g1 = g.astype(BF16)
        r1 = g - g1.astype(F32)
        g2 = r1.astype(BF16)
        g3 = (r1 - g2.astype(F32)).astype(BF16)
        b = _dot(tril, g1) + _dot(tril, g2) + _dot(tril, g3)
        for hh in range(heads):
            bh = b[:, hh * dk:(hh + 1) * dk]
            q = proj_ref[rows, hh * dk:(hh + 1) * dk] * q_scale
            k = proj_ref[rows, nq + hh * dk:nq + (hh + 1) * dk]
            v = proj_ref[rows, 2 * nq + hh * dv:2 * nq + (hh + 1) * dv].astype(BF16)
            r = proj_ref[rows, 2 * nq + nv + hh * dv:2 * nq + nv + (hh + 1) * dv]
            qe = (q * jnp.exp(bh)).astype(BF16)
            ke = (k * jnp.exp(-bh)).astype(BF16)
            att = jnp.where(causal, lax.dot_general(qe, ke, NT, preferred_element_type=F32), 0.0)
            s_t = st_ref[hh]
            o = lax.dot_general(qe, s_t.astype(BF16), NT, preferred_element_type=F32) + _dot(att.astype(BF16), v)
            b_last = bh[chunk - 1:chunk, :]
            k_dec = (k * jnp.exp(b_last - bh)).astype(BF16)
            st_ref[hh] = s_t * jnp.exp(b_last) + lax.dot_general(v, k_dec, TN, preferred_element_type=F32)
            ms = jnp.mean(o * o, axis=-1, keepdims=True)
            on = o * lax.rsqrt(ms + EPS) * gn_ref[...]
            og_ref[rows, hh * dv:(hh + 1) * dv] = (on * _silu(r)).astype(BF16)

    y_ref[0] = x + mod[2:3] * _dot(og_ref[...], wo_ref[...])

    @pl.when(t == pl.num_programs(1) - 1)
    def _():
        for hh in range(heads):
            sout_ref[0, hh] = st_ref[hh].T


def _gla_layer(x, mod, norm_g, w_main, w_a, w_alpha, b_alpha, gla_norm, w_out, s0, *, heads, dk, dv):
    bsz, t, d = x.shape
    tb = min(t, 256)
    chunk = min(CHUNK, t)
    nq, nv = heads * dk, heads * dv
    kern = functools.partial(_gla_kernel, chunk=chunk, heads=heads, dk=dk, dv=dv)
    return pl.pallas_call(
        kern,
        out_shape=(jax.ShapeDtypeStruct((bsz, t, d), F32),
                   jax.ShapeDtypeStruct((bsz, heads, dk, dv), F32)),
        grid=(bsz, t // tb),
        in_specs=[
            pl.BlockSpec((1, tb, d), lambda b, i: (b, i, 0)),
            pl.BlockSpec((1, 6, d), lambda b, i: (b, 0, 0)),
            _resident((1, d)),
            _resident(w_main.shape),
            _resident(w_a.shape),
            _resident(w_alpha.shape),
            _resident((1, nq)),
            _resident((1, dv)),
            _resident(w_out.shape),
            pl.BlockSpec((1, heads, dk, dv), lambda b, i: (b, 0, 0, 0)),
        ],
        out_specs=(pl.BlockSpec((1, tb, d), lambda b, i: (b, i, 0)),
                   pl.BlockSpec((1, heads, dk, dv), lambda b, i: (b, 0, 0, 0))),
        scratch_shapes=[
            pltpu.VMEM((tb, 2 * nq + 2 * nv), F32),
            pltpu.VMEM((tb, nq), F32),
            pltpu.VMEM((tb, nv), BF16),
            pltpu.VMEM((heads, dv, dk), F32),
        ],
        compiler_params=_params("arbitrary", "arbitrary"),
        name="gla_mixer",
    )(x, mod, norm_g.reshape(1, d), w_main, w_a, w_alpha, b_alpha.reshape(1, nq),
      gla_norm.reshape(1, dv), w_out, s0)


def _ffn_kernel(x_ref, mod_ref, ng_ref, wi_ref, cw_ref, cb_ref, wo_ref, cs_ref,
                y_ref, cso_ref, act_ref, *, ff, fc):
    t = pl.program_id(1)
    tb = x_ref.shape[1]

    @pl.when(t == 0)
    def _():
        cso_ref[...] = cs_ref[...]

    x = x_ref[0]
    mod = mod_ref[0]
    hb = _norm_mod(x, ng_ref[...], mod[3:4], mod[4:5]).astype(BF16)
    row = lax.broadcasted_iota(jnp.int32, (tb, fc), 0)
    is0 = row == 0
    is1 = row == 1
    for f0 in range(0, ff, fc):
        conv = []
        for off in (f0, ff + f0):
            cols = slice(off, off + fc)
            up = _dot(hb, wi_ref[:, cols])
            prev = cso_ref[0, :, cols]
            p1 = jnp.where(is0, prev[1:2], pltpu.roll(up, 1, 0))
            p2 = jnp.where(is0, prev[0:1], jnp.where(is1, prev[1:2], pltpu.roll(up, 2, 0)))
            cw = cw_ref[:, cols]
            cv = cb_ref[:, cols] + cw[0:1] * p2
            cv = cv + cw[1:2] * p1
            cv = cv + cw[2:3] * up
            cso_ref[0, :, cols] = up[tb - 2:tb, :]
            conv.append(cv)
        u, gate = conv
        act_ref[:, f0:f0 + fc] = (_silu(gate) * u).astype(BF16)
    y_ref[0] = x + mod[5:6] * _dot(act_ref[...], wo_ref[...])


def _ffn_layer(x, mod, norm_g, w_in, conv_w, conv_b, w_out, conv_state):
    bsz, t, d = x.shape
    ff = w_out.shape[0]
    tb = min(t, 512)
    fc = MXU_DIM
    kern = functools.partial(_ffn_kernel, ff=ff, fc=fc)
    return pl.pallas_call(
        kern,
        out_shape=(jax.ShapeDtypeStruct((bsz, t, d), F32),
                   jax.ShapeDtypeStruct(conv_state.shape, F32)),
        grid=(bsz, t // tb),
        in_specs=[
            pl.BlockSpec((1, tb, d), lambda b, i: (b, i, 0)),
            pl.BlockSpec((1, 6, d), lambda b, i: (b, 0, 0)),
            _resident((1, d)),
            _resident(w_in.shape),
            _resident(conv_w.shape),
            _resident((1, 2 * ff)),
            _resident(w_out.shape),
            pl.BlockSpec((1, 2, 2 * ff), lambda b, i: (b, 0, 0)),
        ],
        out_specs=(pl.BlockSpec((1, tb, d), lambda b, i: (b, i, 0)),
                   pl.BlockSpec((1, 2, 2 * ff), lambda b, i: (b, 0, 0))),
        scratch_shapes=[pltpu.VMEM((tb, ff), BF16)],
        compiler_params=_params("arbitrary", "arbitrary"),
        name="conv_ffn",
    )(x, mod, norm_g.reshape(1, d), w_in, conv_w, conv_b.reshape(1, 2 * ff), w_out, conv_state)


def _dproj_kernel(x_ref, mod_ref, ng_ref, w_ref, qg_ref, kg_ref, gm_ref,
                  qb_ref, kb_ref, vb_ref, k_ref, v_ref, *, hd, transposed):
    x = x_ref[0]
    mod = mod_ref[0]
    hb = _norm_mod(x, ng_ref[...], mod[0:1], mod[1:2]).astype(BF16)
    nqk = qg_ref.shape[1]

    def group_norm(a, gain):
        sq = (a * a).astype(BF16)
        outs = []
        for g0 in range(0, nqk, MXU_DIM):
            ms = _dot(sq[:, g0:g0 + MXU_DIM], gm_ref[...])
            outs.append(a[:, g0:g0 + MXU_DIM] * lax.rsqrt(ms + EPS) * gain[:, g0:g0 + MXU_DIM])
        return jnp.concatenate(outs, axis=-1)

    qn = group_norm(_dot(hb, w_ref[:, 0:nqk]), qg_ref[...]) * (hd ** -0.5)
    kn = group_norm(_dot(hb, w_ref[:, nqk:2 * nqk]), kg_ref[...])
    v = _dot(hb, w_ref[:, 2 * nqk:])
    kb_ref[0] = kn.astype(BF16)
    v_ref[0] = v
    if transposed:
        qb_ref[0] = qn.T.astype(BF16)
        k_ref[0] = kn.T
        vb_ref[0] = v.T.astype(BF16)
    else:
        qb_ref[0] = qn.astype(BF16)
        k_ref[0] = kn
        vb_ref[0] = v.astype(BF16)


def _dproj_layer(x, mod, norm_g, w_in, q_gain, k_gain, gmat, *, hd, transposed):
    bsz, t, d = x.shape
    nqk = q_gain.shape[1]
    nv = w_in.shape[1] - 2 * nqk
    tb = min(t, 512)
    blk = lambda n: pl.BlockSpec((1, tb, n), lambda b, i: (b, i, 0))
    if transposed:
        blk_t = lambda n: pl.BlockSpec((1, n, tb), lambda b, i: (b, 0, i))
        shp_t = lambda n, dt: jax.ShapeDtypeStruct((bsz, n, t), dt)
    else:
        blk_t = blk
        shp_t = lambda n, dt: jax.ShapeDtypeStruct((bsz, t, n), dt)
    return pl.pallas_call(
        functools.partial(_dproj_kernel, hd=hd, transposed=transposed),
        out_shape=(shp_t(nqk, BF16),
                   jax.ShapeDtypeStruct((bsz, t, nqk), BF16),
                   shp_t(nv, BF16),
                   shp_t(nqk, F32),
                   jax.ShapeDtypeStruct((bsz, t, nv), F32)),
        grid=(bsz, t // tb),
        in_specs=[
            blk(d),
            pl.BlockSpec((1, 6, d), lambda b, i: (b, 0, 0)),
            _resident((1, d)),
            _resident(w_in.shape),
            _resident((1, nqk)),
            _resident((1, nqk)),
            _resident(gmat.shape),
        ],
        out_specs=(blk_t(nqk), blk(nqk), blk_t(nv), blk_t(nqk), blk(nv)),
        compiler_params=_params("arbitrary", "arbitrary"),
        name="diff_proj",
    )(x, mod, norm_g.reshape(1, d), w_in, q_gain, k_gain, gmat)


def _attn_init(m_ref, l_ref, acc_ref):
    m_ref[...] = jnp.full(m_ref.shape, NEG, F32)
    l_ref[...] = jnp.zeros(l_ref.shape, F32)
    acc_ref[...] = jnp.zeros(acc_ref.shape, F32)


def _lambda(lam_ref, lam_init):
    lp = lam_ref[...]
    return (jnp.exp(jnp.sum(lp[0:1] * lp[1:2], axis=-1, keepdims=True))
            - jnp.exp(jnp.sum(lp[2:3] * lp[3:4], axis=-1, keepdims=True)) + lam_init)


def _attn_self_kernel(qt_ref, k_ref, vt_ref, x_ref, mod_ref, lam_ref, dn_ref, wo_ref,
                      y_ref, qz_ref, m_ref, l_ref, acc_ref, ob_ref, *, heads, hd, dv, lam_init):
    j = pl.program_id(1)
    tq = qt_ref.shape[2]
    _attn_init(m_ref, l_ref, acc_ref)
    srow = lax.broadcasted_iota(jnp.int32, (2 * hd, tq), 0)
    for h in range(heads):
        qh = qt_ref[0, h * 2 * hd:(h + 1) * 2 * hd, :]
        qz_ref[2 * h] = jnp.where(srow < hd, qh, jnp.zeros_like(qh))
        qz_ref[2 * h + 1] = jnp.where(srow >= hd, qh, jnp.zeros_like(qh))

    def block(start, mask):
        for h in range(heads):
            kh = k_ref[0, pl.ds(start, tq), h * 2 * hd:(h + 1) * 2 * hd]
            vth = vt_ref[0, h * dv:(h + 1) * dv, pl.ds(start, tq)]
            for c in range(2):
                i = 2 * h + c
                s = _dot(kh, qz_ref[i])
                if mask is not None:
                    s = jnp.where(mask, s, NEG)
                m_old = m_ref[i]
                m_new = jnp.maximum(m_old, jnp.max(s, axis=0, keepdims=True))
                alpha = jnp.exp(m_old - m_new)
                p = jnp.exp(s - m_new)
                l_ref[i] = alpha * l_ref[i] + jnp.sum(p, axis=0, keepdims=True)
                acc_ref[i] = alpha * acc_ref[i] + _dot(vth, p.astype(BF16))
                m_ref[i] = m_new

    def body(kb, carry):
        block(pl.multiple_of(kb * tq, tq), None)
        return carry

    lax.fori_loop(0, j, body, 0)
    key = lax.broadcasted_iota(jnp.int32, (tq, tq), 0)
    qry = lax.broadcasted_iota(jnp.int32, (tq, tq), 1)
    block(pl.multiple_of(j * tq, tq), (key // CHUNK) <= (qry // CHUNK))

    lam = _lambda(lam_ref, lam_init)
    for h in range(heads):
        o = acc_ref[2 * h] * (1.0 / l_ref[2 * h]) - lam * (acc_ref[2 * h + 1] * (1.0 / l_ref[2 * h + 1]))
        ms = jnp.mean(o * o, axis=0, keepdims=True)
        on = o * lax.rsqrt(ms + EPS) * dn_ref[...] * (1.0 - lam_init)
        ob_ref[h * dv:(h + 1) * dv, :] = on.astype(BF16)
    mix = lax.dot_general(ob_ref[...], wo_ref[...], TN, preferred_element_type=F32)
    y_ref[0] = x_ref[0] + mod_ref[0][2:3] * mix


def _attn_self_layer(qt, kb, vt, x, mod, lam_p, dnorm, w_out, *, heads, hd, dv, lam_init):
    bsz, t, d = x.shape
    tq = min(t, 256)
    assert t % tq == 0 and tq % CHUNK == 0 and tq % LANES == 0
    kern = functools.partial(_attn_self_kernel, heads=heads, hd=hd, dv=dv, lam_init=lam_init)
    nqk, nv = qt.shape[1], vt.shape[1]
    return pl.pallas_call(
        kern,
        out_shape=jax.ShapeDtypeStruct((bsz, t, d), F32),
        grid=(bsz, t // tq),
        in_specs=[
            pl.BlockSpec((1, nqk, tq), lambda b, i: (b, 0, i)),
            pl.BlockSpec((1, t, nqk), lambda b, i: (b, 0, 0)),
            pl.BlockSpec((1, nv, t), lambda b, i: (b, 0, 0)),
            pl.BlockSpec((1, tq, d), lambda b, i: (b, i, 0)),
            pl.BlockSpec((1, 6, d), lambda b, i: (b, 0, 0)),
            _resident(lam_p.shape),
            _resident((dv, 1)),
            _resident(w_out.shape),
        ],
        out_specs=pl.BlockSpec((1, tq, d), lambda b, i: (b, i, 0)),
        scratch_shapes=[
            pltpu.VMEM((2 * heads, 2 * hd, tq), BF16),
            pltpu.VMEM((2 * heads, 1, tq), F32),
            pltpu.VMEM((2 * heads, 1, tq), F32),
            pltpu.VMEM((2 * heads, dv, tq), F32),
            pltpu.VMEM((heads * dv, tq), BF16),
        ],
        compiler_params=_params("arbitrary", "arbitrary"),
        name="diff_attn_self",
    )(qt, kb, vt, x, mod, lam_p, dnorm.reshape(dv, 1), w_out)


def _attn_cached_kernel(q_ref, kt_ref, vc_ref, kn_ref, vn_ref, x_ref, mod_ref, lam_ref, dn_ref, wo_ref,
                        y_ref, qbd_ref, vb_ref, m_ref, l_ref, acc_ref, ob_ref, *, heads, hd, dv, lam_init):
    kb = pl.program_id(1)
    t = q_ref.shape[1]

    @pl.when(kb == 0)
    def _():
        _attn_init(m_ref, l_ref, acc_ref)
        q = q_ref[0]
        lane = lax.broadcasted_iota(jnp.int32, q.shape, 1)
        for i in range(2 * heads):
            keep = (lane >= i * hd) & (lane < (i + 1) * hd)
            qbd_ref[i * t:(i + 1) * t, :] = jnp.where(keep, q, jnp.zeros_like(q))

    def update(s, v_all):
        m_old = m_ref[...]
        m_new = jnp.maximum(m_old, jnp.max(s, axis=-1, keepdims=True))
        alpha = jnp.exp(m_old - m_new)
        p = jnp.exp(s - m_new)
        l_ref[...] = alpha * l_ref[...] + jnp.sum(p, axis=-1, keepdims=True)
        acc_ref[...] = alpha * acc_ref[...] + _dot(p.astype(BF16), v_all)
        m_ref[...] = m_new

    for h in range(heads):
        vb_ref[:, h * dv:(h + 1) * dv] = vc_ref[0, :, h, :].astype(BF16)
    update(_dot(qbd_ref[...], kt_ref[0].astype(BF16)), vb_ref[...])

    @pl.when(kb == pl.num_programs(1) - 1)
    def _():
        update(lax.dot_general(qbd_ref[...], kn_ref[0], NT, preferred_element_type=F32), vn_ref[0])
        lam = _lambda(lam_ref, lam_init)
        for h in range(heads):
            r0, r1 = 2 * h * t, (2 * h + 1) * t
            cols = slice(h * dv, (h + 1) * dv)
            o = (acc_ref[r0:r0 + t, cols] * (1.0 / l_ref[r0:r0 + t, :])
                 - lam * (acc_ref[r1:r1 + t, cols] * (1.0 / l_ref[r1:r1 + t, :])))
            ms = jnp.mean(o * o, axis=-1, keepdims=True)
            on = o * lax.rsqrt(ms + EPS) * dn_ref[...] * (1.0 - lam_init)
            ob_ref[:, cols] = on.astype(BF16)
        y_ref[0] = x_ref[0] + mod_ref[0][2:3] * _dot(ob_ref[...], wo_ref[...])


def _attn_cached_layer(qb, kt_cache, v_cache, kb, vb, x, mod, lam_p, dnorm, w_out, *, heads, hd, dv, lam_init):
    bsz, t, d = x.shape
    nqk, past = kt_cache.shape[1:]
    tk = min(past, 512)
    assert past % tk == 0 and past % CHUNK == 0 and t <= CHUNK and t % 8 == 0
    kern = functools.partial(_attn_cached_kernel, heads=heads, hd=hd, dv=dv, lam_init=lam_init)
    per_seq = lambda n: pl.BlockSpec((1, t, n), lambda b, i: (b, 0, 0))
    rows = 2 * heads * t
    return pl.pallas_call(
        kern,
        out_shape=jax.ShapeDtypeStruct((bsz, t, d), F32),
        grid=(bsz, past // tk),
        in_specs=[
            per_seq(nqk),
            pl.BlockSpec((1, nqk, tk), lambda b, i: (b, 0, i)),
            pl.BlockSpec((1, tk, heads, dv), lambda b, i: (b, i, 0, 0)),
            per_seq(nqk),
            per_seq(heads * dv),
            per_seq(d),
            pl.BlockSpec((1, 6, d), lambda b, i: (b, 0, 0)),
            _resident(lam_p.shape),
            _resident((1, dv)),
            _resident(w_out.shape),
        ],
        out_specs=per_seq(d),
        scratch_shapes=[
            pltpu.VMEM((rows, nqk), BF16),
            pltpu.VMEM((tk, heads * dv), BF16),
            pltpu.VMEM((rows, 1), F32),
            pltpu.VMEM((rows, 1), F32),
            pltpu.VMEM((rows, heads * dv), F32),
            pltpu.VMEM((t, heads * dv), BF16),
        ],
        compiler_params=_params("arbitrary", "arbitrary"),
        name="diff_attn_cached",
    )(qb, kt_cache, v_cache, kb, vb, x, mod, lam_p, dnorm.reshape(1, dv), w_out)


def kernel(x_prompt, x_sample, state_gla, cache_diff_k, cache_diff_v, state_ffn_conv, c_prompt, c_sample,
           norm_mix, norm_ffn, w_ada, b_ada, gla_w_in, gla_w_alpha, gla_b_alpha, gla_norm, gla_w_out,
           diff_w_in, diff_q_norm, diff_k_norm, diff_lambda, diff_norm, diff_w_out,
           ffn_w_in, ffn_conv_w, ffn_conv_b, ffn_w_out):
    depth, d = norm_mix.shape
    bp, bs = x_prompt.shape[0], x_sample.shape[0]
    ff = ffn_w_out.shape[1]

    rank, nq = gla_w_alpha.shape[1:]
    g_dv = gla_norm.shape[1]
    g_heads = gla_w_out.shape[1] // g_dv
    g_dk = nq // g_heads
    n_main = gla_w_in.shape[2] - rank
    hd = diff_q_norm.shape[1]
    a_dv = diff_norm.shape[1]
    a_heads = diff_w_out.shape[1] // a_dv
    nqk = a_heads * 2 * hd
    assert 2 * hd == LANES and a_dv == LANES and MXU_DIM % hd == 0

    mods = _adaln(jnp.concatenate([c_prompt, c_sample], axis=0), w_ada, b_ada)
    mods = mods.reshape(depth, bp + bs, 6, d)
    group_rows = (slice(0, bp), slice(bp, bp + bs))

    gi = jnp.arange(MXU_DIM) // hd
    gmat = jnp.where(gi[:, None] == gi[None, :], 1.0 / hd, 0.0).astype(BF16)

    xs = [x_prompt, x_sample]
    new_gla, new_k, new_v, new_conv = [[], []], [[], []], [[], []], [[], []]
    for i in range(depth):
        j = i // 2
        lam_init = 0.8 - 0.6 * math.exp(-0.3 * i)
        f_in = ffn_w_in[i].astype(BF16)
        f_out = ffn_w_out[i].astype(BF16)
        if i % 2 == 0:
            w_main = gla_w_in[j][:, :n_main].astype(BF16)
            w_a = jnp.pad(gla_w_in[j][:, n_main:], ((0, 0), (0, LANES - rank))).astype(BF16)
            w_alpha = jnp.pad(gla_w_alpha[j], ((0, LANES - rank), (0, 0))).astype(BF16)
            m_out = gla_w_out[j].astype(BF16)
        else:
            m_in = diff_w_in[j].astype(BF16)
            m_out = diff_w_out[j].astype(BF16)
            q_gain = jnp.tile(diff_q_norm[j], 2 * a_heads).reshape(1, nqk)
            k_gain = jnp.tile(diff_k_norm[j], 2 * a_heads).reshape(1, nqk)
        for grp in range(2):
            x = xs[grp]
            bsz, t, _ = x.shape
            mod = mods[i, group_rows[grp]]
            if i % 2 == 0:
                s0 = jnp.zeros((bsz, g_heads, g_dk, g_dv), F32) if grp == 0 else state_gla[j]
                x, s_fin = _gla_layer(x, mod, norm_mix[i], w_main, w_a, w_alpha, gla_b_alpha[j], gla_norm[j],
                                      m_out, s0, heads=g_heads, dk=g_dk, dv=g_dv)
                new_gla[grp].append(s_fin)
            else:
                qb, kb, vb, k, v = _dproj_layer(x, mod, norm_mix[i], m_in, q_gain, k_gain, gmat, hd=hd,
                                                transposed=(grp == 0))
                common = dict(heads=a_heads, hd=hd, dv=a_dv, lam_init=lam_init)
                if grp == 0:
                    x = _attn_self_layer(qb, kb, vb, x, mod, diff_lambda[j], diff_norm[j], m_out, **common)
                    k = jnp.swapaxes(k, 1, 2)
                else:
                    past = cache_diff_k.shape[2]
                    kt_cache = jnp.swapaxes(cache_diff_k[j].reshape(bsz, past, nqk), 1, 2)
                    x = _attn_cached_layer(qb, kt_cache, cache_diff_v[j], kb, vb, x, mod,
                                           diff_lambda[j], diff_norm[j], m_out, **common)
                new_k[grp].append(k.reshape(bsz, t, a_heads, 2, hd))
                new_v[grp].append(v.reshape(bsz, t, a_heads, a_dv))
            cst = jnp.zeros((bsz, 2, 2 * ff), F32) if grp == 0 else state_ffn_conv[i]
            x, cst_new = _ffn_layer(x, mod, norm_ffn[i], f_in, ffn_conv_w[i], ffn_conv_b[i], f_out, cst)
            new_conv[grp].append(cst_new)
            xs[grp] = x

    stack = lambda parts: parts[0][None] if len(parts) == 1 else jnp.stack(parts)
    return (xs[0], xs[1],
            stack(new_gla[0]), stack(new_gla[1]),
            stack(new_k[0]), stack(new_k[1]),
            stack(new_v[0]), stack(new_v[1]),
            stack(new_conv[0]), stack(new_conv[1]))
```

```python
import functools
import math

import jax
import jax.numpy as jnp
from jax import lax
from jax.experimental import pallas as pl
from jax.experimental.pallas import tpu as pltpu

F32 = jnp.float32
BF16 = jnp.bfloat16

EPS = 1e-6
CHUNK = 64
GLA_TAU = 16.0
LANES = 128
MXU_DIM = 256
NEG = -1e30
LOG2E = math.log2(math.e)
VMEM_LIMIT = 56 * 1024 * 1024

NT = (((1,), (1,)), ((), ()))
TN = (((0,), (0,)), ((), ()))


def _dot(a, b):
    return jnp.dot(a, b, preferred_element_type=F32)


def _silu(x):
    return x * (1.0 / (1.0 + jnp.exp(-x)))


def _norm_mod(x, gain, shift, scale):
    ms = jnp.mean(x * x, axis=-1, keepdims=True)
    y = x * lax.rsqrt(ms + EPS) * gain
    return y * (1.0 + scale) + shift


def _resident(shape):
    zeros = (0,) * len(shape)
    return pl.BlockSpec(shape, lambda *_: zeros, pipeline_mode=pl.Buffered(1))


def _params(*sem):
    return pltpu.CompilerParams(dimension_semantics=sem, vmem_limit_bytes=VMEM_LIMIT)


def _adaln_kernel(c_ref, w_ref, b_ref, o_ref):
    s = _silu(c_ref[...]).astype(BF16)
    o_ref[0] = _dot(s, w_ref[0].astype(BF16)) + b_ref[0]


def _adaln(c, w, b):
    depth, d, n = w.shape
    rows = c.shape[0]
    tn = n // 4
    return pl.pallas_call(
        _adaln_kernel,
        out_shape=jax.ShapeDtypeStruct((depth, rows, n), F32),
        grid=(depth, n // tn),
        in_specs=[
            pl.BlockSpec((rows, d), lambda i, j: (0, 0)),
            pl.BlockSpec((1, d, tn), lambda i, j: (i, 0, j)),
            pl.BlockSpec((1, 1, tn), lambda i, j: (i, 0, j)),
        ],
        out_specs=pl.BlockSpec((1, rows, tn), lambda i, j: (i, 0, j)),
        compiler_params=_params("arbitrary", "arbitrary"),
        name="adaln",
    )(c, w, b.reshape(depth, 1, n))


def _gla_kernel(x_ref, mod_ref, ng_ref, wm_ref, wa_ref, wal_ref, bal_ref, gn_ref, wo_ref, s0_ref,
                y_ref, sout_ref, proj_ref, g_ref, og_ref, st_ref, vb_ref, qe_ref, ke_ref, kd_ref, dec_ref, kv_ref,
                *, chunk, heads, dk, dv):
    t = pl.program_id(1)
    tb = x_ref.shape[1]
    nq, nv = heads * dk, heads * dv

    @pl.when(t == 0)
    def _():
        for hh in range(heads):
            st_ref[hh] = s0_ref[0, hh].T

    x = x_ref[0]
    mod = mod_ref[0]
    hb = _norm_mod(x, ng_ref[...], mod[0:1], mod[1:2]).astype(BF16)
    proj_ref[...] = _dot(hb, wm_ref[...])
    alow = _dot(hb, wa_ref[...])
    logit = _dot(alow.astype(BF16), wal_ref[...]) + bal_ref[...]
    g_ref[...] = (jnp.minimum(logit, 0.0) - jnp.log1p(jnp.exp(-jnp.abs(logit)))) / GLA_TAU

    row = lax.broadcasted_iota(jnp.int32, (chunk, chunk), 0)
    col = lax.broadcasted_iota(jnp.int32, (chunk, chunk), 1)
    causal = row >= col
    tril = jnp.where(causal, 1.0, 0.0).astype(BF16)
    q_scale = dk ** -0.5
    n_chunks = tb // chunk
    units = [(c, hh) for c in range(n_chunks) for hh in range(heads)]
    vb_ref[...] = proj_ref[:, 2 * nq:2 * nq + nv].astype(BF16)

    for c in range(n_chunks):
        rows = pl.ds(c * chunk, chunk)
        g = g_ref[rows, :]
        g1 = g.astype(BF16)
        r1 = g - g1.astype(F32)
        g2 = r1.astype(BF16)
        g3 = (r1 - g2.astype(F32)).astype(BF16)
        b = _dot(tril, g1) + _dot(tril, g2) + _dot(tril, g3)
        b_last = b[chunk - 1:chunk, :]
        q = proj_ref[rows, 0:nq] * q_scale
        k = proj_ref[rows, nq:2 * nq]
        qe_ref[rows, :] = (q * jnp.exp(b)).astype(BF16)
        ke_ref[rows, :] = (k * jnp.exp(-b)).astype(BF16)
        kd_ref[rows, :] = (k * jnp.exp(b_last - b)).astype(BF16)
        dec_ref[c] = jnp.exp(b_last)

    def operand(ref, c, hh, width):
        return ref[pl.ds(c * chunk, chunk), hh * width:(hh + 1) * width]

    att = {}
    for c, hh in units:
        a = lax.dot_general(operand(qe_ref, c, hh, dk), operand(ke_ref, c, hh, dk), NT,
                            preferred_element_type=F32)
        att[c, hh] = jnp.where(causal, a, 0.0).astype(BF16)
    for i, (c, hh) in enumerate(units):
        kv_ref[i] = lax.dot_general(operand(vb_ref, c, hh, dv), operand(kd_ref, c, hh, dk), TN,
                                    preferred_element_type=F32)

    for i, (c, hh) in enumerate(units):
        rows = pl.ds(c * chunk, chunk)
        s_t = st_ref[hh]
        o = (lax.dot_general(operand(qe_ref, c, hh, dk), s_t.astype(BF16), NT, preferred_element_type=F32)
             + _dot(att[c, hh], operand(vb_ref, c, hh, dv)))
        st_ref[hh] = s_t * dec_ref[c][:, hh * dk:(hh + 1) * dk] + kv_ref[i]
        r = proj_ref[rows, 2 * nq + nv + hh * dv:2 * nq + nv + (hh + 1) * dv]
        ms = jnp.mean(o * o, axis=-1, keepdims=True)
        on = o * lax.rsqrt(ms + EPS) * gn_ref[...]
        og_ref[rows, hh * dv:(hh + 1) * dv] = (on * _silu(r)).astype(BF16)

    y_ref[0] = x + mod[2:3] * _dot(og_ref[...], wo_ref[...])

    @pl.when(t == pl.num_programs(1) - 1)
    def _():
        for hh in range(heads):
            sout_ref[0, hh] = st_ref[hh].T


def _gla_layer(x, mod, norm_g, w_main, w_a, w_alpha, b_alpha, gla_norm, w_out, s0, *, heads, dk, dv):
    bsz, t, d = x.shape
    tb = min(t, 256)
    chunk = min(CHUNK, t)
    nq, nv = heads * dk, heads * dv
    kern = functools.partial(_gla_kernel, chunk=chunk, heads=heads, dk=dk, dv=dv)
    return pl.pallas_call(
        kern,
        out_shape=(jax.ShapeDtypeStruct((bsz, t, d), F32),
                   jax.ShapeDtypeStruct((bsz, heads, dk, dv), F32)),
        grid=(bsz, t // tb),
        in_specs=[
            pl.BlockSpec((1, tb, d), lambda b, i: (b, i, 0)),
            pl.BlockSpec((1, 6, d), lambda b, i: (b, 0, 0)),
            _resident((1, d)),
            _resident(w_main.shape),
            _resident(w_a.shape),
            _resident(w_alpha.shape),
            _resident((1, nq)),
            _resident((1, dv)),
            _resident(w_out.shape),
            pl.BlockSpec((1, heads, dk, dv), lambda b, i: (b, 0, 0, 0)),
        ],
        out_specs=(pl.BlockSpec((1, tb, d), lambda b, i: (b, i, 0)),
                   pl.BlockSpec((1, heads, dk, dv), lambda b, i: (b, 0, 0, 0))),
        scratch_shapes=[
            pltpu.VMEM((tb, 2 * nq + 2 * nv), F32),
            pltpu.VMEM((tb, nq), F32),
            pltpu.VMEM((tb, nv), BF16),
            pltpu.VMEM((heads, dv, dk), F32),
            pltpu.VMEM((tb, nv), BF16),
            pltpu.VMEM((tb, nq), BF16),
            pltpu.VMEM((tb, nq), BF16),
            pltpu.VMEM((tb, nq), BF16),
            pltpu.VMEM((tb // chunk, 1, nq), F32),
            pltpu.VMEM((tb // chunk * heads, dv, dk), F32),
        ],
        compiler_params=_params("arbitrary", "arbitrary"),
        name="gla_mixer",
    )(x, mod, norm_g.reshape(1, d), w_main, w_a, w_alpha, b_alpha.reshape(1, nq),
      gla_norm.reshape(1, dv), w_out, s0)


def _ffn_kernel(x_ref, mod_ref, ng_ref, wi_ref, cw_ref, cb_ref, wo_ref, cs_ref,
                y_ref, cso_ref, act_ref, *, ff, fc):
    t = pl.program_id(1)
    tb = x_ref.shape[1]

    @pl.when(t == 0)
    def _():
        cso_ref[...] = cs_ref[...]

    x = x_ref[0]
    mod = mod_ref[0]
    hb = _norm_mod(x, ng_ref[...], mod[3:4], mod[4:5]).astype(BF16)
    row = lax.broadcasted_iota(jnp.int32, (tb, fc), 0)
    is0 = row == 0
    is1 = row == 1
    for f0 in range(0, ff, fc):
        conv = []
        for off in (f0, ff + f0):
            cols = slice(off, off + fc)
            up = _dot(hb, wi_ref[:, cols])
            prev = cso_ref[0, :, cols]
            p1 = jnp.where(is0, prev[1:2], pltpu.roll(up, 1, 0))
            p2 = jnp.where(is0, prev[0:1], jnp.where(is1, prev[1:2], pltpu.roll(up, 2, 0)))
            cw = cw_ref[:, cols]
            cv = cb_ref[:, cols] + cw[0:1] * p2
            cv = cv + cw[1:2] * p1
            cv = cv + cw[2:3] * up
            cso_ref[0, :, cols] = up[tb - 2:tb, :]
            conv.append(cv)
        u, gate = conv
        act_ref[:, f0:f0 + fc] = (_silu(gate) * u).astype(BF16)
    y_ref[0] = x + mod[5:6] * _dot(act_ref[...], wo_ref[...])


def _ffn_layer(x, mod, norm_g, w_in, conv_w, conv_b, w_out, conv_state):
    bsz, t, d = x.shape
    ff = w_out.shape[0]
    tb = min(t, 512)
    fc = MXU_DIM
    kern = functools.partial(_ffn_kernel, ff=ff, fc=fc)
    return pl.pallas_call(
        kern,
        out_shape=(jax.ShapeDtypeStruct((bsz, t, d), F32),
                   jax.ShapeDtypeStruct(conv_state.shape, F32)),
        grid=(bsz, t // tb),
        in_specs=[
            pl.BlockSpec((1, tb, d), lambda b, i: (b, i, 0)),
            pl.BlockSpec((1, 6, d), lambda b, i: (b, 0, 0)),
            _resident((1, d)),
            _resident(w_in.shape),
            _resident(conv_w.shape),
            _resident((1, 2 * ff)),
            _resident(w_out.shape),
            pl.BlockSpec((1, 2, 2 * ff), lambda b, i: (b, 0, 0)),
        ],
        out_specs=(pl.BlockSpec((1, tb, d), lambda b, i: (b, i, 0)),
                   pl.BlockSpec((1, 2, 2 * ff), lambda b, i: (b, 0, 0))),
        scratch_shapes=[pltpu.VMEM((tb, ff), BF16)],
        compiler_params=_params("arbitrary", "arbitrary"),
        name="conv_ffn",
    )(x, mod, norm_g.reshape(1, d), w_in, conv_w, conv_b.reshape(1, 2 * ff), w_out, conv_state)


def _dproj_kernel(x_ref, mod_ref, ng_ref, w_ref, qg_ref, kg_ref, gm_ref,
                  qb_ref, kb_ref, vb_ref, k_ref, v_ref, *, hd, transposed):
    x = x_ref[0]
    mod = mod_ref[0]
    hb = _norm_mod(x, ng_ref[...], mod[0:1], mod[1:2]).astype(BF16)
    nqk = qg_ref.shape[1]

    def group_norm(a, gain):
        sq = (a * a).astype(BF16)
        outs = []
        for g0 in range(0, nqk, MXU_DIM):
            ms = _dot(sq[:, g0:g0 + MXU_DIM], gm_ref[...])
            outs.append(a[:, g0:g0 + MXU_DIM] * lax.rsqrt(ms + EPS) * gain[:, g0:g0 + MXU_DIM])
        return jnp.concatenate(outs, axis=-1)

    qn = group_norm(_dot(hb, w_ref[:, 0:nqk]), qg_ref[...]) * (hd ** -0.5 * LOG2E)
    kn = group_norm(_dot(hb, w_ref[:, nqk:2 * nqk]), kg_ref[...])
    v = _dot(hb, w_ref[:, 2 * nqk:])
    kb_ref[0] = kn.astype(BF16)
    v_ref[0] = v
    if transposed:
        qb_ref[0] = qn.T.astype(BF16)
        k_ref[0] = kn.T
        vb_ref[0] = v.T.astype(BF16)
    else:
        qb_ref[0] = qn.astype(BF16)
        k_ref[0] = kn
        vb_ref[0] = v.astype(BF16)


def _dproj_layer(x, mod, norm_g, w_in, q_gain, k_gain, gmat, *, hd, transposed):
    bsz, t, d = x.shape
    nqk = q_gain.shape[1]
    nv = w_in.shape[1] - 2 * nqk
    tb = min(t, 512)
    blk = lambda n: pl.BlockSpec((1, tb, n), lambda b, i: (b, i, 0))
    if transposed:
        blk_t = lambda n: pl.BlockSpec((1, n, tb), lambda b, i: (b, 0, i))
        shp_t = lambda n, dt: jax.ShapeDtypeStruct((bsz, n, t), dt)
    else:
        blk_t = blk
        shp_t = lambda n, dt: jax.ShapeDtypeStruct((bsz, t, n), dt)
    return pl.pallas_call(
        functools.partial(_dproj_kernel, hd=hd, transposed=transposed),
        out_shape=(shp_t(nqk, BF16),
                   jax.ShapeDtypeStruct((bsz, t, nqk), BF16),
                   shp_t(nv, BF16),
                   shp_t(nqk, F32),
                   jax.ShapeDtypeStruct((bsz, t, nv), F32)),
        grid=(bsz, t // tb),
        in_specs=[
            blk(d),
            pl.BlockSpec((1, 6, d), lambda b, i: (b, 0, 0)),
            _resident((1, d)),
            _resident(w_in.shape),
            _resident((1, nqk)),
            _resident((1, nqk)),
            _resident(gmat.shape),
        ],
        out_specs=(blk_t(nqk), blk(nqk), blk_t(nv), blk_t(nqk), blk(nv)),
        compiler_params=_params("arbitrary", "arbitrary"),
        name="diff_proj",
    )(x, mod, norm_g.reshape(1, d), w_in, q_gain, k_gain, gmat)


def _attn_init(m_ref, l_ref, acc_ref):
    m_ref[...] = jnp.full(m_ref.shape, NEG, F32)
    l_ref[...] = jnp.zeros(l_ref.shape, F32)
    acc_ref[...] = jnp.zeros(acc_ref.shape, F32)


def _lambda(lam_ref, lam_init):
    lp = lam_ref[...]
    return (jnp.exp(jnp.sum(lp[0:1] * lp[1:2], axis=-1, keepdims=True))
            - jnp.exp(jnp.sum(lp[2:3] * lp[3:4], axis=-1, keepdims=True)) + lam_init)


SCORE_LOOKAHEAD = 6
SUM_ROWS = 16


def _attn_self_kernel(qt_ref, k_ref, vt_ref, x_ref, mod_ref, lam_ref, dn_ref, wo_ref,
                      y_ref, qz_ref, m_ref, acc_ref, ob_ref, s_ref, *, heads, hd, dv, lam_init):
    j = pl.program_id(1)
    tq = qt_ref.shape[2]
    n = 2 * heads
    n_slot = s_ref.shape[0]
    m_ref[...] = jnp.full(m_ref.shape, NEG, F32)
    acc_ref[...] = jnp.zeros(acc_ref.shape, F32)
    srow = lax.broadcasted_iota(jnp.int32, (2 * hd, tq), 0)
    for h in range(heads):
        qh = qt_ref[0, h * 2 * hd:(h + 1) * 2 * hd, :]
        qz_ref[2 * h] = jnp.where(srow < hd, qh, jnp.zeros_like(qh))
        qz_ref[2 * h + 1] = jnp.where(srow >= hd, qh, jnp.zeros_like(qh))
    ones = jnp.ones((SUM_ROWS, tq), BF16)

    def block(start, mask):
        def scores(i):
            kh = k_ref[0, pl.ds(start, tq), (i // 2) * 2 * hd:(i // 2 + 1) * 2 * hd]
            s = _dot(kh, qz_ref[i])
            if mask is not None:
                s = jnp.where(mask, s, NEG)
            s_ref[i % n_slot] = s

        def update(i):
            h = i // 2
            vth = jnp.concatenate([vt_ref[0, h * dv:(h + 1) * dv, pl.ds(start, tq)], ones], axis=0)
            s = s_ref[i % n_slot]
            m_old = m_ref[i]
            m_new = jnp.maximum(m_old, jnp.max(s, axis=0, keepdims=True))
            p = jnp.exp2(s - m_new)
            acc_ref[i] = jnp.exp2(m_old - m_new) * acc_ref[i] + _dot(vth, p.astype(BF16))
            m_ref[i] = m_new

        for i in range(min(SCORE_LOOKAHEAD, n)):
            scores(i)
        for i in range(n):
            update(i)
            if i + SCORE_LOOKAHEAD < n:
                scores(i + SCORE_LOOKAHEAD)

    def body(kb, carry):
        block(pl.multiple_of(kb * tq, tq), None)
        return carry

    lax.fori_loop(0, j, body, 0)
    key = lax.broadcasted_iota(jnp.int32, (tq, tq), 0)
    qry = lax.broadcasted_iota(jnp.int32, (tq, tq), 1)
    block(pl.multiple_of(j * tq, tq), (key // CHUNK) <= (qry // CHUNK))

    lam = _lambda(lam_ref, lam_init)
    for h in range(heads):
        a0 = acc_ref[2 * h]
        a1 = acc_ref[2 * h + 1]
        o = a0[0:dv] * (1.0 / a0[dv:dv + 1]) - lam * (a1[0:dv] * (1.0 / a1[dv:dv + 1]))
        ms = jnp.mean(o * o, axis=0, keepdims=True)
        on = o * lax.rsqrt(ms + EPS) * dn_ref[...] * (1.0 - lam_init)
        ob_ref[h * dv:(h + 1) * dv, :] = on.astype(BF16)
    mix = lax.dot_general(ob_ref[...], wo_ref[...], TN, preferred_element_type=F32)
    y_ref[0] = x_ref[0] + mod_ref[0][2:3] * mix


def _attn_self_layer(qt, kb, vt, x, mod, lam_p, dnorm, w_out, *, heads, hd, dv, lam_init):
    bsz, t, d = x.shape
    tq = min(t, 256)
    assert t % tq == 0 and tq % CHUNK == 0 and tq % LANES == 0
    kern = functools.partial(_attn_self_kernel, heads=heads, hd=hd, dv=dv, lam_init=lam_init)
    nqk, nv = qt.shape[1], vt.shape[1]
    return pl.pallas_call(
        kern,
        out_shape=jax.ShapeDtypeStruct((bsz, t, d), F32),
        grid=(bsz, t // tq),
        in_specs=[
            pl.BlockSpec((1, nqk, tq), lambda b, i: (b, 0, i)),
            pl.BlockSpec((1, t, nqk), lambda b, i: (b, 0, 0)),
            pl.BlockSpec((1, nv, t), lambda b, i: (b, 0, 0)),
            pl.BlockSpec((1, tq, d), lambda b, i: (b, i, 0)),
            pl.BlockSpec((1, 6, d), lambda b, i: (b, 0, 0)),
            _resident(lam_p.shape),
            _resident((dv, 1)),
            _resident(w_out.shape),
        ],
        out_specs=pl.BlockSpec((1, tq, d), lambda b, i: (b, i, 0)),
        scratch_shapes=[
            pltpu.VMEM((2 * heads, 2 * hd, tq), BF16),
            pltpu.VMEM((2 * heads, 1, tq), F32),
            pltpu.VMEM((2 * heads, dv + SUM_ROWS, tq), F32),
            pltpu.VMEM((heads * dv, tq), BF16),
            pltpu.VMEM((SCORE_LOOKAHEAD + 1, tq, tq), F32),
        ],
        compiler_params=_params("arbitrary", "arbitrary"),
        name="diff_attn_self",
    )(qt, kb, vt, x, mod, lam_p, dnorm.reshape(dv, 1), w_out)


def _attn_cached_kernel(q_ref, kt_ref, vc_ref, kn_ref, vn_ref, x_ref, mod_ref, lam_ref, dn_ref, wo_ref,
                        y_ref, qbd_ref, vb_ref, m_ref, l_ref, acc_ref, ob_ref, *, heads, hd, dv, lam_init):
    kb = pl.program_id(1)
    t = q_ref.shape[1]

    @pl.when(kb == 0)
    def _():
        _attn_init(m_ref, l_ref, acc_ref)
        q = q_ref[0]
        lane = lax.broadcasted_iota(jnp.int32, q.shape, 1)
        for i in range(2 * heads):
            keep = (lane >= i * hd) & (lane < (i + 1) * hd)
            qbd_ref[i * t:(i + 1) * t, :] = jnp.where(keep, q, jnp.zeros_like(q))

    def update(s, v_all):
        m_old = m_ref[...]
        m_new = jnp.maximum(m_old, jnp.max(s, axis=-1, keepdims=True))
        alpha = jnp.exp2(m_old - m_new)
        p = jnp.exp2(s - m_new)
        l_ref[...] = alpha * l_ref[...] + jnp.sum(p, axis=-1, keepdims=True)
        acc_ref[...] = alpha * acc_ref[...] + _dot(p.astype(BF16), v_all)
        m_ref[...] = m_new

    for h in range(heads):
        vb_ref[:, h * dv:(h + 1) * dv] = vc_ref[0, :, h, :].astype(BF16)
    update(_dot(qbd_ref[...], kt_ref[0].astype(BF16)), vb_ref[...])

    @pl.when(kb == pl.num_programs(1) - 1)
    def _():
        update(lax.dot_general(qbd_ref[...], kn_ref[0], NT, preferred_element_type=F32), vn_ref[0])
        lam = _lambda(lam_ref, lam_init)
        for h in range(heads):
            r0, r1 = 2 * h * t, (2 * h + 1) * t
            cols = slice(h * dv, (h + 1) * dv)
            o = (acc_ref[r0:r0 + t, cols] * (1.0 / l_ref[r0:r0 + t, :])
                 - lam * (acc_ref[r1:r1 + t, cols] * (1.0 / l_ref[r1:r1 + t, :])))
            ms = jnp.mean(o * o, axis=-1, keepdims=True)
            on = o * lax.rsqrt(ms + EPS) * dn_ref[...] * (1.0 - lam_init)
            ob_ref[:, cols] = on.astype(BF16)
        y_ref[0] = x_ref[0] + mod_ref[0][2:3] * _dot(ob_ref[...], wo_ref[...])


def _attn_cached_layer(qb, kt_cache, v_cache, kb, vb, x, mod, lam_p, dnorm, w_out, *, heads, hd, dv, lam_init):
    bsz, t, d = x.shape
    nqk, past = kt_cache.shape[1:]
    tk = min(past, 512)
    assert past % tk == 0 and past % CHUNK == 0 and t <= CHUNK and t % 8 == 0
    kern = functools.partial(_attn_cached_kernel, heads=heads, hd=hd, dv=dv, lam_init=lam_init)
    per_seq = lambda n: pl.BlockSpec((1, t, n), lambda b, i: (b, 0, 0))
    rows = 2 * heads * t
    return pl.pallas_call(
        kern,
        out_shape=jax.ShapeDtypeStruct((bsz, t, d), F32),
        grid=(bsz, past // tk),
        in_specs=[
            per_seq(nqk),
            pl.BlockSpec((1, nqk, tk), lambda b, i: (b, 0, i)),
            pl.BlockSpec((1, tk, heads, dv), lambda b, i: (b, i, 0, 0)),
            per_seq(nqk),
            per_seq(heads * dv),
            per_seq(d),
            pl.BlockSpec((1, 6, d), lambda b, i: (b, 0, 0)),
            _resident(lam_p.shape),
            _resident((1, dv)),
            _resident(w_out.shape),
        ],
        out_specs=per_seq(d),
        scratch_shapes=[
            pltpu.VMEM((rows, nqk), BF16),
            pltpu.VMEM((tk, heads * dv), BF16),
            pltpu.VMEM((rows, 1), F32),
            pltpu.VMEM((rows, 1), F32),
            pltpu.VMEM((rows, heads * dv), F32),
            pltpu.VMEM((t, heads * dv), BF16),
        ],
        compiler_params=_params("arbitrary", "arbitrary"),
        name="diff_attn_cached",
    )(qb, kt_cache, v_cache, kb, vb, x, mod, lam_p, dnorm.reshape(1, dv), w_out)


def kernel(x_prompt, x_sample, state_gla, cache_diff_k, cache_diff_v, state_ffn_conv, c_prompt, c_sample,
           norm_mix, norm_ffn, w_ada, b_ada, gla_w_in, gla_w_alpha, gla_b_alpha, gla_norm, gla_w_out,
           diff_w_in, diff_q_norm, diff_k_norm, diff_lambda, diff_norm, diff_w_out,
           ffn_w_in, ffn_conv_w, ffn_conv_b, ffn_w_out):
    depth, d = norm_mix.shape
    bp, bs = x_prompt.shape[0], x_sample.shape[0]
    ff = ffn_w_out.shape[1]

    rank, nq = gla_w_alpha.shape[1:]
    g_dv = gla_norm.shape[1]
    g_heads = gla_w_out.shape[1] // g_dv
    g_dk = nq // g_heads
    n_main = gla_w_in.shape[2] - rank
    hd = diff_q_norm.shape[1]
    a_dv = diff_norm.shape[1]
    a_heads = diff_w_out.shape[1] // a_dv
    nqk = a_heads * 2 * hd
    assert 2 * hd == LANES and a_dv == LANES and MXU_DIM % hd == 0

    mods = _adaln(jnp.concatenate([c_prompt, c_sample], axis=0), w_ada, b_ada)
    mods = mods.reshape(depth, bp + bs, 6, d)
    group_rows = (slice(0, bp), slice(bp, bp + bs))

    gi = jnp.arange(MXU_DIM) // hd
    gmat = jnp.where(gi[:, None] == gi[None, :], 1.0 / hd, 0.0).astype(BF16)

    xs = [x_prompt, x_sample]
    new_gla, new_k, new_v, new_conv = [[], []], [[], []], [[], []], [[], []]
    for i in range(depth):
        j = i // 2
        lam_init = 0.8 - 0.6 * math.exp(-0.3 * i)
        f_in = ffn_w_in[i].astype(BF16)
        f_out = ffn_w_out[i].astype(BF16)
        if i % 2 == 0:
            w_main = gla_w_in[j][:, :n_main].astype(BF16)
            w_a = jnp.pad(gla_w_in[j][:, n_main:], ((0, 0), (0, LANES - rank))).astype(BF16)
            w_alpha = jnp.pad(gla_w_alpha[j], ((0, LANES - rank), (0, 0))).astype(BF16)
            m_out = gla_w_out[j].astype(BF16)
        else:
            m_in = diff_w_in[j].astype(BF16)
            m_out = diff_w_out[j].astype(BF16)
            q_gain = jnp.tile(diff_q_norm[j], 2 * a_heads).reshape(1, nqk)
            k_gain = jnp.tile(diff_k_norm[j], 2 * a_heads).reshape(1, nqk)
        for grp in range(2):
            x = xs[grp]
            bsz, t, _ = x.shape
            mod = mods[i, group_rows[grp]]
            if i % 2 == 0:
                s0 = jnp.zeros((bsz, g_heads, g_dk, g_dv), F32) if grp == 0 else state_gla[j]
                x, s_fin = _gla_layer(x, mod, norm_mix[i], w_main, w_a, w_alpha, gla_b_alpha[j], gla_norm[j],
                                      m_out, s0, heads=g_heads, dk=g_dk, dv=g_dv)
                new_gla[grp].append(s_fin)
            else:
                qb, kb, vb, k, v = _dproj_layer(x, mod, norm_mix[i], m_in, q_gain, k_gain, gmat, hd=hd,
                                                transposed=(grp == 0))
                common = dict(heads=a_heads, hd=hd, dv=a_dv, lam_init=lam_init)
                if grp == 0:
                    x = _attn_self_layer(qb, kb, vb, x, mod, diff_lambda[j], diff_norm[j], m_out, **common)
                    k = jnp.swapaxes(k, 1, 2)
                else:
                    past = cache_diff_k.shape[2]
                    kt_cache = jnp.swapaxes(cache_diff_k[j].reshape(bsz, past, nqk), 1, 2)
                    x = _attn_cached_layer(qb, kt_cache, cache_diff_v[j], kb, vb, x, mod,
                                           diff_lambda[j], diff_norm[j], m_out, **common)
                new_k[grp].append(k.reshape(bsz, t, a_heads, 2, hd))
                new_v[grp].append(v.reshape(bsz, t, a_heads, a_dv))
            cst = jnp.zeros((bsz, 2, 2 * ff), F32) if grp == 0 else state_ffn_conv[i]
            x, cst_new = _ffn_layer(x, mod, norm_ffn[i], f_in, ffn_conv_w[i], ffn_conv_b[i], f_out, cst)
            new_conv[grp].append(cst_new)
            xs[grp] = x

    stack = lambda parts: parts[0][None] if len(parts) == 1 else jnp.stack(parts)
    return (xs[0], xs[1],
            stack(new_gla[0]), stack(new_gla[1]),
            stack(new_k[0]), stack(new_k[1]),
            stack(new_v[0]), stack(new_v[1]),
            stack(new_conv[0]), stack(new_conv[1]))
```

```python
import functools
import math

import jax
import jax.numpy as jnp
from jax import lax
from jax.experimental import pallas as pl
from jax.experimental.pallas import tpu as pltpu

F32 = jnp.float32
BF16 = jnp.bfloat16

EPS = 1e-6
CHUNK = 64
GLA_TAU = 16.0
LANES = 128
MXU_DIM = 256
NEG = -1e30
LOG2E = math.log2(math.e)
VMEM_LIMIT = 56 * 1024 * 1024

NT = (((1,), (1,)), ((), ()))
TN = (((0,), (0,)), ((), ()))


def _dot(a, b):
    return jnp.dot(a, b, preferred_element_type=F32)


def _silu(x):
    return x * (1.0 / (1.0 + jnp.exp(-x)))


def _norm_mod(x, gain, shift, scale):
    ms = jnp.mean(x * x, axis=-1, keepdims=True)
    y = x * lax.rsqrt(ms + EPS) * gain
    return y * (1.0 + scale) + shift


def _resident(shape):
    zeros = (0,) * len(shape)
    return pl.BlockSpec(shape, lambda *_: zeros, pipeline_mode=pl.Buffered(1))


def _params(*sem):
    return pltpu.CompilerParams(dimension_semantics=sem, vmem_limit_bytes=VMEM_LIMIT)


def _adaln_kernel(c_ref, w_ref, b_ref, o_ref):
    s = _silu(c_ref[...]).astype(BF16)
    o_ref[0] = _dot(s, w_ref[0].astype(BF16)) + b_ref[0]


def _adaln(c, w, b):
    depth, d, n = w.shape
    rows = c.shape[0]
    tn = n // 4
    return pl.pallas_call(
        _adaln_kernel,
        out_shape=jax.ShapeDtypeStruct((depth, rows, n), F32),
        grid=(depth, n // tn),
        in_specs=[
            pl.BlockSpec((rows, d), lambda i, j: (0, 0)),
            pl.BlockSpec((1, d, tn), lambda i, j: (i, 0, j)),
            pl.BlockSpec((1, 1, tn), lambda i, j: (i, 0, j)),
        ],
        out_specs=pl.BlockSpec((1, rows, tn), lambda i, j: (i, 0, j)),
        compiler_params=_params("arbitrary", "arbitrary"),
        name="adaln",
    )(c, w, b.reshape(depth, 1, n))


def _gla_kernel(x_ref, mod_ref, ng_ref, wm_ref, wa_ref, wal_ref, bal_ref, gn_ref, wo_ref, s0_ref,
                y_ref, sout_ref, proj_ref, g_ref, og_ref, st_ref, vb_ref, qe_ref, ke_ref, kd_ref, dec_ref, kv_ref,
                *, chunk, heads, dk, dv):
    t = pl.program_id(1)
    tb = x_ref.shape[1]
    nq, nv = heads * dk, heads * dv

    @pl.when(t == 0)
    def _():
        for hh in range(heads):
            st_ref[hh] = s0_ref[0, hh].T

    x = x_ref[0]
    mod = mod_ref[0]
    hb = _norm_mod(x, ng_ref[...], mod[0:1], mod[1:2]).astype(BF16)
    proj_ref[...] = _dot(hb, wm_ref[...])
    alow = _dot(hb, wa_ref[...])
    logit = _dot(alow.astype(BF16), wal_ref[...]) + bal_ref[...]
    g_ref[...] = (jnp.minimum(logit, 0.0) - jnp.log1p(jnp.exp(-jnp.abs(logit)))) / GLA_TAU

    row = lax.broadcasted_iota(jnp.int32, (chunk, chunk), 0)
    col = lax.broadcasted_iota(jnp.int32, (chunk, chunk), 1)
    causal = row >= col
    tril = jnp.where(causal, 1.0, 0.0).astype(BF16)
    q_scale = dk ** -0.5
    n_chunks = tb // chunk
    units = [(c, hh) for c in range(n_chunks) for hh in range(heads)]
    vb_ref[...] = proj_ref[:, 2 * nq:2 * nq + nv].astype(BF16)

    for c in range(n_chunks):
        rows = pl.ds(c * chunk, chunk)
        g = g_ref[rows, :]
        g1 = g.astype(BF16)
        r1 = g - g1.astype(F32)
        g2 = r1.astype(BF16)
        g3 = (r1 - g2.astype(F32)).astype(BF16)
        b = _dot(tril, g1) + _dot(tril, g2) + _dot(tril, g3)
        b_last = b[chunk - 1:chunk, :]
        q = proj_ref[rows, 0:nq] * q_scale
        k = proj_ref[rows, nq:2 * nq]
        qe_ref[rows, :] = (q * jnp.exp(b)).astype(BF16)
        ke_ref[rows, :] = (k * jnp.exp(-b)).astype(BF16)
        kd_ref[rows, :] = (k * jnp.exp(b_last - b)).astype(BF16)
        dec_ref[c] = jnp.exp(b_last)

    def operand(ref, c, hh, width):
        return ref[pl.ds(c * chunk, chunk), hh * width:(hh + 1) * width]

    att = {}
    for c, hh in units:
        a = lax.dot_general(operand(qe_ref, c, hh, dk), operand(ke_ref, c, hh, dk), NT,
                            preferred_element_type=F32)
        att[c, hh] = jnp.where(causal, a, 0.0).astype(BF16)
    for i, (c, hh) in enumerate(units):
        kv_ref[i] = lax.dot_general(operand(vb_ref, c, hh, dv), operand(kd_ref, c, hh, dk), TN,
                                    preferred_element_type=F32)

    for i, (c, hh) in enumerate(units):
        rows = pl.ds(c * chunk, chunk)
        s_t = st_ref[hh]
        o = (lax.dot_general(operand(qe_ref, c, hh, dk), s_t.astype(BF16), NT, preferred_element_type=F32)
             + _dot(att[c, hh], operand(vb_ref, c, hh, dv)))
        st_ref[hh] = s_t * dec_ref[c][:, hh * dk:(hh + 1) * dk] + kv_ref[i]
        r = proj_ref[rows, 2 * nq + nv + hh * dv:2 * nq + nv + (hh + 1) * dv]
        ms = jnp.mean(o * o, axis=-1, keepdims=True)
        on = o * lax.rsqrt(ms + EPS) * gn_ref[...]
        og_ref[rows, hh * dv:(hh + 1) * dv] = (on * _silu(r)).astype(BF16)

    y_ref[0] = x + mod[2:3] * _dot(og_ref[...], wo_ref[...])

    @pl.when(t == pl.num_programs(1) - 1)
    def _():
        for hh in range(heads):
            sout_ref[0, hh] = st_ref[hh].T


def _gla_layer(x, mod, norm_g, w_main, w_a, w_alpha, b_alpha, gla_norm, w_out, s0, *, heads, dk, dv):
    bsz, t, d = x.shape
    tb = min(t, 512)
    chunk = min(CHUNK, t)
    nq, nv = heads * dk, heads * dv
    kern = functools.partial(_gla_kernel, chunk=chunk, heads=heads, dk=dk, dv=dv)
    return pl.pallas_call(
        kern,
        out_shape=(jax.ShapeDtypeStruct((bsz, t, d), F32),
                   jax.ShapeDtypeStruct((bsz, heads, dk, dv), F32)),
        grid=(bsz, t // tb),
        in_specs=[
            pl.BlockSpec((1, tb, d), lambda b, i: (b, i, 0)),
            pl.BlockSpec((1, 6, d), lambda b, i: (b, 0, 0)),
            _resident((1, d)),
            _resident(w_main.shape),
            _resident(w_a.shape),
            _resident(w_alpha.shape),
            _resident((1, nq)),
            _resident((1, dv)),
            _resident(w_out.shape),
            pl.BlockSpec((1, heads, dk, dv), lambda b, i: (b, 0, 0, 0)),
        ],
        out_specs=(pl.BlockSpec((1, tb, d), lambda b, i: (b, i, 0)),
                   pl.BlockSpec((1, heads, dk, dv), lambda b, i: (b, 0, 0, 0))),
        scratch_shapes=[
            pltpu.VMEM((tb, 2 * nq + 2 * nv), F32),
            pltpu.VMEM((tb, nq), F32),
            pltpu.VMEM((tb, nv), BF16),
            pltpu.VMEM((heads, dv, dk), F32),
            pltpu.VMEM((tb, nv), BF16),
            pltpu.VMEM((tb, nq), BF16),
            pltpu.VMEM((tb, nq), BF16),
            pltpu.VMEM((tb, nq), BF16),
            pltpu.VMEM((tb // chunk, 1, nq), F32),
            pltpu.VMEM((tb // chunk * heads, dv, dk), F32),
        ],
        compiler_params=_params("arbitrary", "arbitrary"),
        name="gla_mixer",
    )(x, mod, norm_g.reshape(1, d), w_main, w_a, w_alpha, b_alpha.reshape(1, nq),
      gla_norm.reshape(1, dv), w_out, s0)


def _ffn_kernel(x_ref, mod_ref, ng_ref, wi_ref, cw_ref, cb_ref, wo_ref, cs_ref,
                y_ref, cso_ref, act_ref, *, ff, fc):
    t = pl.program_id(1)
    tb = x_ref.shape[1]

    @pl.when(t == 0)
    def _():
        cso_ref[...] = cs_ref[...]

    x = x_ref[0]
    mod = mod_ref[0]
    hb = _norm_mod(x, ng_ref[...], mod[3:4], mod[4:5]).astype(BF16)
    row = lax.broadcasted_iota(jnp.int32, (tb, fc), 0)
    is0 = row == 0
    is1 = row == 1
    for f0 in range(0, ff, fc):
        conv = []
        for off in (f0, ff + f0):
            cols = slice(off, off + fc)
            up = _dot(hb, wi_ref[:, cols])
            prev = cso_ref[0, :, cols]
            p1 = jnp.where(is0, prev[1:2], pltpu.roll(up, 1, 0))
            p2 = jnp.where(is0, prev[0:1], jnp.where(is1, prev[1:2], pltpu.roll(up, 2, 0)))
            cw = cw_ref[:, cols]
            cv = cb_ref[:, cols] + cw[0:1] * p2
            cv = cv + cw[1:2] * p1
            cv = cv + cw[2:3] * up
            cso_ref[0, :, cols] = up[tb - 2:tb, :]
            conv.append(cv)
        u, gate = conv
        act_ref[:, f0:f0 + fc] = (_silu(gate) * u).astype(BF16)
    y_ref[0] = x + mod[5:6] * _dot(act_ref[...], wo_ref[...])


def _ffn_layer(x, mod, norm_g, w_in, conv_w, conv_b, w_out, conv_state):
    bsz, t, d = x.shape
    ff = w_out.shape[0]
    tb = min(t, 512)
    fc = MXU_DIM
    kern = functools.partial(_ffn_kernel, ff=ff, fc=fc)
    return pl.pallas_call(
        kern,
        out_shape=(jax.ShapeDtypeStruct((bsz, t, d), F32),
                   jax.ShapeDtypeStruct(conv_state.shape, F32)),
        grid=(bsz, t // tb),
        in_specs=[
            pl.BlockSpec((1, tb, d), lambda b, i: (b, i, 0)),
            pl.BlockSpec((1, 6, d), lambda b, i: (b, 0, 0)),
            _resident((1, d)),
            _resident(w_in.shape),
            _resident(conv_w.shape),
            _resident((1, 2 * ff)),
            _resident(w_out.shape),
            pl.BlockSpec((1, 2, 2 * ff), lambda b, i: (b, 0, 0)),
        ],
        out_specs=(pl.BlockSpec((1, tb, d), lambda b, i: (b, i, 0)),
                   pl.BlockSpec((1, 2, 2 * ff), lambda b, i: (b, 0, 0))),
        scratch_shapes=[pltpu.VMEM((tb, ff), BF16)],
        compiler_params=_params("arbitrary", "arbitrary"),
        name="conv_ffn",
    )(x, mod, norm_g.reshape(1, d), w_in, conv_w, conv_b.reshape(1, 2 * ff), w_out, conv_state)


def _dproj_kernel(x_ref, mod_ref, ng_ref, w_ref, qg_ref, kg_ref, gm_ref,
                  qb_ref, kb_ref, vb_ref, k_ref, v_ref, *, hd, transposed):
    x = x_ref[0]
    mod = mod_ref[0]
    hb = _norm_mod(x, ng_ref[...], mod[0:1], mod[1:2]).astype(BF16)
    nqk = qg_ref.shape[1]

    def group_norm(a, gain):
        sq = (a * a).astype(BF16)
        outs = []
        for g0 in range(0, nqk, MXU_DIM):
            ms = _dot(sq[:, g0:g0 + MXU_DIM], gm_ref[...])
            outs.append(a[:, g0:g0 + MXU_DIM] * lax.rsqrt(ms + EPS) * gain[:, g0:g0 + MXU_DIM])
        return jnp.concatenate(outs, axis=-1)

    qn = group_norm(_dot(hb, w_ref[:, 0:nqk]), qg_ref[...]) * (hd ** -0.5 * LOG2E)
    kn = group_norm(_dot(hb, w_ref[:, nqk:2 * nqk]), kg_ref[...])
    v = _dot(hb, w_ref[:, 2 * nqk:])
    kb_ref[0] = kn.astype(BF16)
    v_ref[0] = v
    if transposed:
        qb_ref[0] = qn.T.astype(BF16)
        k_ref[0] = kn.T
        vb_ref[0] = v.T.astype(BF16)
    else:
        qb_ref[0] = qn.astype(BF16)
        k_ref[0] = kn
        vb_ref[0] = v.astype(BF16)


def _dproj_layer(x, mod, norm_g, w_in, q_gain, k_gain, gmat, *, hd, transposed):
    bsz, t, d = x.shape
    nqk = q_gain.shape[1]
    nv = w_in.shape[1] - 2 * nqk
    tb = min(t, 512)
    blk = lambda n: pl.BlockSpec((1, tb, n), lambda b, i: (b, i, 0))
    if transposed:
        blk_t = lambda n: pl.BlockSpec((1, n, tb), lambda b, i: (b, 0, i))
        shp_t = lambda n, dt: jax.ShapeDtypeStruct((bsz, n, t), dt)
    else:
        blk_t = blk
        shp_t = lambda n, dt: jax.ShapeDtypeStruct((bsz, t, n), dt)
    return pl.pallas_call(
        functools.partial(_dproj_kernel, hd=hd, transposed=transposed),
        out_shape=(shp_t(nqk, BF16),
                   jax.ShapeDtypeStruct((bsz, t, nqk), BF16),
                   shp_t(nv, BF16),
                   shp_t(nqk, F32),
                   jax.ShapeDtypeStruct((bsz, t, nv), F32)),
        grid=(bsz, t // tb),
        in_specs=[
            blk(d),
            pl.BlockSpec((1, 6, d), lambda b, i: (b, 0, 0)),
            _resident((1, d)),
            _resident(w_in.shape),
            _resident((1, nqk)),
            _resident((1, nqk)),
            _resident(gmat.shape),
        ],
        out_specs=(blk_t(nqk), blk(nqk), blk_t(nv), blk_t(nqk), blk(nv)),
        compiler_params=_params("arbitrary", "arbitrary"),
        name="diff_proj",
    )(x, mod, norm_g.reshape(1, d), w_in, q_gain, k_gain, gmat)


def _attn_init(m_ref, l_ref, acc_ref):
    m_ref[...] = jnp.full(m_ref.shape, NEG, F32)
    l_ref[...] = jnp.zeros(l_ref.shape, F32)
    acc_ref[...] = jnp.zeros(acc_ref.shape, F32)


def _lambda(lam_ref, lam_init):
    lp = lam_ref[...]
    return (jnp.exp(jnp.sum(lp[0:1] * lp[1:2], axis=-1, keepdims=True))
            - jnp.exp(jnp.sum(lp[2:3] * lp[3:4], axis=-1, keepdims=True)) + lam_init)


SCORE_LOOKAHEAD = 6
SUM_ROWS = 16


def _attn_self_kernel(qt_ref, k_ref, vt_ref, x_ref, mod_ref, lam_ref, dn_ref, wo_ref,
                      y_ref, qz_ref, m_ref, acc_ref, ob_ref, s_ref, *, heads, hd, dv, lam_init):
    j = pl.program_id(1)
    tq = qt_ref.shape[2]
    n = 2 * heads
    n_slot = s_ref.shape[0]
    m_ref[...] = jnp.full(m_ref.shape, NEG, F32)
    acc_ref[...] = jnp.zeros(acc_ref.shape, F32)
    srow = lax.broadcasted_iota(jnp.int32, (2 * hd, tq), 0)
    for h in range(heads):
        qh = qt_ref[0, h * 2 * hd:(h + 1) * 2 * hd, :]
        qz_ref[2 * h] = jnp.where(srow < hd, qh, jnp.zeros_like(qh))
        qz_ref[2 * h + 1] = jnp.where(srow >= hd, qh, jnp.zeros_like(qh))
    ones = jnp.ones((SUM_ROWS, tq), BF16)

    def block(start, mask):
        def scores(i):
            kh = k_ref[0, pl.ds(start, tq), (i // 2) * 2 * hd:(i // 2 + 1) * 2 * hd]
            s = _dot(kh, qz_ref[i])
            if mask is not None:
                s = jnp.where(mask, s, NEG)
            s_ref[i % n_slot] = s

        def update(i):
            h = i // 2
            vth = jnp.concatenate([vt_ref[0, h * dv:(h + 1) * dv, pl.ds(start, tq)], ones], axis=0)
            s = s_ref[i % n_slot]
            m_old = m_ref[i]
            m_new = jnp.maximum(m_old, jnp.max(s, axis=0, keepdims=True))
            p = jnp.exp2(s - m_new)
            acc_ref[i] = jnp.exp2(m_old - m_new) * acc_ref[i] + _dot(vth, p.astype(BF16))
            m_ref[i] = m_new

        for i in range(min(SCORE_LOOKAHEAD, n)):
            scores(i)
        for i in range(n):
            update(i)
            if i + SCORE_LOOKAHEAD < n:
                scores(i + SCORE_LOOKAHEAD)

    def body(kb, carry):
        block(pl.multiple_of(kb * tq, tq), None)
        return carry

    lax.fori_loop(0, j, body, 0)
    key = lax.broadcasted_iota(jnp.int32, (tq, tq), 0)
    qry = lax.broadcasted_iota(jnp.int32, (tq, tq), 1)
    block(pl.multiple_of(j * tq, tq), (key // CHUNK) <= (qry // CHUNK))

    lam = _lambda(lam_ref, lam_init)
    for h in range(heads):
        a0 = acc_ref[2 * h]
        a1 = acc_ref[2 * h + 1]
        o = a0[0:dv] * (1.0 / a0[dv:dv + 1]) - lam * (a1[0:dv] * (1.0 / a1[dv:dv + 1]))
        ms = jnp.mean(o * o, axis=0, keepdims=True)
        on = o * lax.rsqrt(ms + EPS) * dn_ref[...] * (1.0 - lam_init)
        ob_ref[h * dv:(h + 1) * dv, :] = on.astype(BF16)
    mix = lax.dot_general(ob_ref[...], wo_ref[...], TN, preferred_element_type=F32)
    y_ref[0] = x_ref[0] + mod_ref[0][2:3] * mix


def _attn_self_layer(qt, kb, vt, x, mod, lam_p, dnorm, w_out, *, heads, hd, dv, lam_init):
    bsz, t, d = x.shape
    tq = min(t, 256)
    assert t % tq == 0 and tq % CHUNK == 0 and tq % LANES == 0
    kern = functools.partial(_attn_self_kernel, heads=heads, hd=hd, dv=dv, lam_init=lam_init)
    nqk, nv = qt.shape[1], vt.shape[1]
    return pl.pallas_call(
        kern,
        out_shape=jax.ShapeDtypeStruct((bsz, t, d), F32),
        grid=(bsz, t // tq),
        in_specs=[
            pl.BlockSpec((1, nqk, tq), lambda b, i: (b, 0, i)),
            pl.BlockSpec((1, t, nqk), lambda b, i: (b, 0, 0)),
            pl.BlockSpec((1, nv, t), lambda b, i: (b, 0, 0)),
            pl.BlockSpec((1, tq, d), lambda b, i: (b, i, 0)),
            pl.BlockSpec((1, 6, d), lambda b, i: (b, 0, 0)),
            _resident(lam_p.shape),
            _resident((dv, 1)),
            _resident(w_out.shape),
        ],
        out_specs=pl.BlockSpec((1, tq, d), lambda b, i: (b, i, 0)),
        scratch_shapes=[
            pltpu.VMEM((2 * heads, 2 * hd, tq), BF16),
            pltpu.VMEM((2 * heads, 1, tq), F32),
            pltpu.VMEM((2 * heads, dv + SUM_ROWS, tq), F32),
            pltpu.VMEM((heads * dv, tq), BF16),
            pltpu.VMEM((SCORE_LOOKAHEAD + 1, tq, tq), F32),
        ],
        compiler_params=_params("arbitrary", "arbitrary"),
        name="diff_attn_self",
    )(qt, kb, vt, x, mod, lam_p, dnorm.reshape(dv, 1), w_out)


def _attn_cached_kernel(q_ref, kt_ref, vc_ref, kn_ref, vn_ref, x_ref, mod_ref, lam_ref, dn_ref, wo_ref,
                        y_ref, qbd_ref, vb_ref, m_ref, l_ref, acc_ref, ob_ref, *, heads, hd, dv, lam_init):
    kb = pl.program_id(1)
    t = q_ref.shape[1]

    @pl.when(kb == 0)
    def _():
        _attn_init(m_ref, l_ref, acc_ref)
        q = q_ref[0]
        lane = lax.broadcasted_iota(jnp.int32, q.shape, 1)
        for i in range(2 * heads):
            keep = (lane >= i * hd) & (lane < (i + 1) * hd)
            qbd_ref[i * t:(i + 1) * t, :] = jnp.where(keep, q, jnp.zeros_like(q))

    def update(s, v_all):
        m_old = m_ref[...]
        m_new = jnp.maximum(m_old, jnp.max(s, axis=-1, keepdims=True))
        alpha = jnp.exp2(m_old - m_new)
        p = jnp.exp2(s - m_new)
        l_ref[...] = alpha * l_ref[...] + jnp.sum(p, axis=-1, keepdims=True)
        acc_ref[...] = alpha * acc_ref[...] + _dot(p.astype(BF16), v_all)
        m_ref[...] = m_new

    for h in range(heads):
        vb_ref[:, h * dv:(h + 1) * dv] = vc_ref[0, :, h, :].astype(BF16)
    update(_dot(qbd_ref[...], kt_ref[0].astype(BF16)), vb_ref[...])

    @pl.when(kb == pl.num_programs(1) - 1)
    def _():
        update(lax.dot_general(qbd_ref[...], kn_ref[0], NT, preferred_element_type=F32), vn_ref[0])
        lam = _lambda(lam_ref, lam_init)
        for h in range(heads):
            r0, r1 = 2 * h * t, (2 * h + 1) * t
            cols = slice(h * dv, (h + 1) * dv)
            o = (acc_ref[r0:r0 + t, cols] * (1.0 / l_ref[r0:r0 + t, :])
                 - lam * (acc_ref[r1:r1 + t, cols] * (1.0 / l_ref[r1:r1 + t, :])))
            ms = jnp.mean(o * o, axis=-1, keepdims=True)
            on = o * lax.rsqrt(ms + EPS) * dn_ref[...] * (1.0 - lam_init)
            ob_ref[:, cols] = on.astype(BF16)
        y_ref[0] = x_ref[0] + mod_ref[0][2:3] * _dot(ob_ref[...], wo_ref[...])


def _attn_cached_layer(qb, kt_cache, v_cache, kb, vb, x, mod, lam_p, dnorm, w_out, *, heads, hd, dv, lam_init):
    bsz, t, d = x.shape
    nqk, past = kt_cache.shape[1:]
    tk = min(past, 512)
    assert past % tk == 0 and past % CHUNK == 0 and t <= CHUNK and t % 8 == 0
    kern = functools.partial(_attn_cached_kernel, heads=heads, hd=hd, dv=dv, lam_init=lam_init)
    per_seq = lambda n: pl.BlockSpec((1, t, n), lambda b, i: (b, 0, 0))
    rows = 2 * heads * t
    return pl.pallas_call(
        kern,
        out_shape=jax.ShapeDtypeStruct((bsz, t, d), F32),
        grid=(bsz, past // tk),
        in_specs=[
            per_seq(nqk),
            pl.BlockSpec((1, nqk, tk), lambda b, i: (b, 0, i)),
            pl.BlockSpec((1, tk, heads, dv), lambda b, i: (b, i, 0, 0)),
            per_seq(nqk),
            per_seq(heads * dv),
            per_seq(d),
            pl.BlockSpec((1, 6, d), lambda b, i: (b, 0, 0)),
            _resident(lam_p.shape),
            _resident((1, dv)),
            _resident(w_out.shape),
        ],
        out_specs=per_seq(d),
        scratch_shapes=[
            pltpu.VMEM((rows, nqk), BF16),
            pltpu.VMEM((tk, heads * dv), BF16),
            pltpu.VMEM((rows, 1), F32),
            pltpu.VMEM((rows, 1), F32),
            pltpu.VMEM((rows, heads * dv), F32),
            pltpu.VMEM((t, heads * dv), BF16),
        ],
        compiler_params=_params("arbitrary", "arbitrary"),
        name="diff_attn_cached",
    )(qb, kt_cache, v_cache, kb, vb, x, mod, lam_p, dnorm.reshape(1, dv), w_out)


def kernel(x_prompt, x_sample, state_gla, cache_diff_k, cache_diff_v, state_ffn_conv, c_prompt, c_sample,
           norm_mix, norm_ffn, w_ada, b_ada, gla_w_in, gla_w_alpha, gla_b_alpha, gla_norm, gla_w_out,
           diff_w_in, diff_q_norm, diff_k_norm, diff_lambda, diff_norm, diff_w_out,
           ffn_w_in, ffn_conv_w, ffn_conv_b, ffn_w_out):
    depth, d = norm_mix.shape
    bp, bs = x_prompt.shape[0], x_sample.shape[0]
    ff = ffn_w_out.shape[1]

    rank, nq = gla_w_alpha.shape[1:]
    g_dv = gla_norm.shape[1]
    g_heads = gla_w_out.shape[1] // g_dv
    g_dk = nq // g_heads
    n_main = gla_w_in.shape[2] - rank
    hd = diff_q_norm.shape[1]
    a_dv = diff_norm.shape[1]
    a_heads = diff_w_out.shape[1] // a_dv
    nqk = a_heads * 2 * hd
    assert 2 * hd == LANES and a_dv == LANES and MXU_DIM % hd == 0

    mods = _adaln(jnp.concatenate([c_prompt, c_sample], axis=0), w_ada, b_ada)
    mods = mods.reshape(depth, bp + bs, 6, d)
    group_rows = (slice(0, bp), slice(bp, bp + bs))

    gi = jnp.arange(MXU_DIM) // hd
    gmat = jnp.where(gi[:, None] == gi[None, :], 1.0 / hd, 0.0).astype(BF16)

    xs = [x_prompt, x_sample]
    new_gla, new_k, new_v, new_conv = [[], []], [[], []], [[], []], [[], []]
    for i in range(depth):
        j = i // 2
        lam_init = 0.8 - 0.6 * math.exp(-0.3 * i)
        f_in = ffn_w_in[i].astype(BF16)
        f_out = ffn_w_out[i].astype(BF16)
        if i % 2 == 0:
            w_main = gla_w_in[j][:, :n_main].astype(BF16)
            w_a = jnp.pad(gla_w_in[j][:, n_main:], ((0, 0), (0, LANES - rank))).astype(BF16)
            w_alpha = jnp.pad(gla_w_alpha[j], ((0, LANES - rank), (0, 0))).astype(BF16)
            m_out = gla_w_out[j].astype(BF16)
        else:
            m_in = diff_w_in[j].astype(BF16)
            m_out = diff_w_out[j].astype(BF16)
            q_gain = jnp.tile(diff_q_norm[j], 2 * a_heads).reshape(1, nqk)
            k_gain = jnp.tile(diff_k_norm[j], 2 * a_heads).reshape(1, nqk)
        for grp in range(2):
            x = xs[grp]
            bsz, t, _ = x.shape
            mod = mods[i, group_rows[grp]]
            if i % 2 == 0:
                s0 = jnp.zeros((bsz, g_heads, g_dk, g_dv), F32) if grp == 0 else state_gla[j]
                x, s_fin = _gla_layer(x, mod, norm_mix[i], w_main, w_a, w_alpha, gla_b_alpha[j], gla_norm[j],
                                      m_out, s0, heads=g_heads, dk=g_dk, dv=g_dv)
                new_gla[grp].append(s_fin)
            else:
                qb, kb, vb, k, v = _dproj_layer(x, mod, norm_mix[i], m_in, q_gain, k_gain, gmat, hd=hd,
                                                transposed=(grp == 0))
                common = dict(heads=a_heads, hd=hd, dv=a_dv, lam_init=lam_init)
                if grp == 0:
                    x = _attn_self_layer(qb, kb, vb, x, mod, diff_lambda[j], diff_norm[j], m_out, **common)
                    k = jnp.swapaxes(k, 1, 2)
                else:
                    past = cache_diff_k.shape[2]
                    kt_cache = jnp.swapaxes(cache_diff_k[j].reshape(bsz, past, nqk), 1, 2)
                    x = _attn_cached_layer(qb, kt_cache, cache_diff_v[j], kb, vb, x, mod,
                                           diff_lambda[j], diff_norm[j], m_out, **common)
                new_k[grp].append(k.reshape(bsz, t, a_heads, 2, hd))
                new_v[grp].append(v.reshape(bsz, t, a_heads, a_dv))
            cst = jnp.zeros((bsz, 2, 2 * ff), F32) if grp == 0 else state_ffn_conv[i]
            x, cst_new = _ffn_layer(x, mod, norm_ffn[i], f_in, ffn_conv_w[i], ffn_conv_b[i], f_out, cst)
            new_conv[grp].append(cst_new)
            xs[grp] = x

    stack = lambda parts: parts[0][None] if len(parts) == 1 else jnp.stack(parts)
    return (xs[0], xs[1],
            stack(new_gla[0]), stack(new_gla[1]),
            stack(new_k[0]), stack(new_k[1]),
            stack(new_v[0]), stack(new_v[1]),
            stack(new_conv[0]), stack(new_conv[1]))
```
